```python
import math
import jax
import jax.numpy as jnp
from jax import lax
import numpy as np

D_MODEL = 1024
BATCH = 8
SEQ = 4096
DEPTH = 2

D_MIX = D_MODEL
N_MIXERS = 4
G_WIDTH = D_MIX // N_MIXERS
HEAD_DIM = 64
G_HEADS = G_WIDTH // HEAD_DIM
N_DIR = 2
CONV_K = 4
LRU_BLOCKS = G_HEADS
LRU_C = 8.0
CHUNK = 64
GATE_CAP = 15.0
RWKV_W_RANK = 32
RWKV_A_RANK = 32
RWKV_G_RANK = 64
RWKV_GN_EPS = 64e-5
RMS_EPS = 1e-6
D_FF = -(-(8 * D_MODEL) // (3 * 256)) * 256

A_COLS = 2 * G_WIDTH
B_COLS = 4 * G_WIDTH + 2 * N_DIR * G_HEADS
C_COLS = 4 * G_WIDTH + 2 * N_DIR * G_HEADS
D_COLS = 3 * G_WIDTH + N_DIR * RWKV_W_RANK + N_DIR * RWKV_A_RANK + RWKV_G_RANK
P_IN = A_COLS + B_COLS + C_COLS + D_COLS

kernel_name = 'hymba_style_bidir_hybrid_block'


def rms_norm(x, g, eps=RMS_EPS):
    xf = x.astype(jnp.float32)
    y = xf * lax.rsqrt(jnp.mean(xf * xf, axis=-1, keepdims=True) + eps)
    return (y * g.astype(jnp.float32)).astype(x.dtype)


def head_rms(x, eps=RMS_EPS):
    return x * lax.rsqrt(jnp.mean(x * x, axis=-1, keepdims=True) + eps)


def l2_normalize(x, eps=1e-6):
    return x * lax.rsqrt(jnp.sum(x * x, axis=-1, keepdims=True) + eps)


def soft_cap(x):
    return GATE_CAP * jnp.tanh(x / GATE_CAP)


def flip_seq(t):
    return jnp.flip(t, axis=1)


def split_heads(t):
    return t.reshape(*t.shape[:-1], G_HEADS, HEAD_DIM)


def conv_centred(x, w):
    k_w, ch = w.shape
    return lax.conv_general_dilated(
        x, w[:, None, :].astype(x.dtype), window_strides=(1,),
        padding=[(k_w // 2, k_w - 1 - k_w // 2)],
        dimension_numbers=('NWC', 'WIO', 'NWC'), feature_group_count=ch)


def to_chunks(t, size):
    bsz, seq, nh = t.shape[:3]
    t = t.reshape(bsz, seq // size, size, nh, *t.shape[3:])
    return jnp.moveaxis(t, 3, 1)


def from_chunks(o):
    nc, bsz, nh, size, d = o.shape
    return jnp.transpose(o, (1, 0, 3, 2, 4)).reshape(bsz, nc * size, nh, d)


def linear_scan(a, b, reverse):
    def combine(left, right):
        return left[0] * right[0], right[0] * left[1] + right[1]
    return lax.associative_scan(combine, (a, b), reverse=reverse, axis=1)[1]


def rglru_mixer(u, conv_w, conv_b, gate_w, gate_b, lam):
    bsz, seq, _ = u.shape
    xb, yb = jnp.split(u, 2, axis=-1)
    xc = conv_centred(xb, conv_w) + conv_b
    xr = xc.reshape(bsz, seq, LRU_BLOCKS, G_WIDTH // LRU_BLOCKS)
    pre = jnp.einsum('bsni,dgnio->bsdgno', xr, gate_w).reshape(bsz, seq, N_DIR, 2, G_WIDTH) + gate_b
    r = jax.nn.sigmoid(pre[:, :, :, 0])
    i = jax.nn.sigmoid(pre[:, :, :, 1])
    log_a = -LRU_C * r * jax.nn.softplus(-lam)
    a = jnp.exp(log_a)
    b = jnp.sqrt(-jnp.expm1(2.0 * log_a)) * i * xc[:, :, None, :]
    h = linear_scan(a[:, :, 0], b[:, :, 0], False) + linear_scan(a[:, :, 1], b[:, :, 1], True)
    return h * jax.nn.gelu(yb)


def gated_delta_chunked(q, k, v, g, beta):
    dk = q.shape[-1]
    dv = v.shape[-1]
    q = q * dk ** -0.5
    q, k, v = (to_chunks(t, CHUNK) for t in (q, k, v))
    g, beta = (to_chunks(t, CHUNK) for t in (g, beta))
    gc = jnp.cumsum(g, axis=-1)
    tri = jnp.tril(jnp.ones((CHUNK, CHUNK), bool))
    strict = jnp.tril(jnp.ones((CHUNK, CHUNK), bool), -1)
    decay = jnp.exp(jnp.where(tri, gc[..., :, None] - gc[..., None, :], -jnp.inf))
    kb = k * beta[..., None]
    lower = jnp.where(strict, jnp.einsum('bhcld,bhcmd->bhclm', kb, k) * decay, 0.0)
    a_mat = lower + jnp.eye(CHUNK, dtype=lower.dtype)
    rhs = jnp.concatenate([v * beta[..., None], kb * jnp.exp(gc)[..., None]], axis=-1)
    sol = lax.linalg.triangular_solve(a_mat, rhs, left_side=True, lower=True, unit_diagonal=True)
    u_c, w_c = sol[..., :dv], sol[..., dv:]
    qk = jnp.einsum('bhcld,bhcmd->bhclm', q, k) * decay
    xs = tuple(jnp.moveaxis(t, 2, 0) for t in (q, k, u_c, w_c, qk, gc))

    def step(state, inp):
        q_c, k_c, u_i, w_i, qk_c, g_c = inp
        v_new = u_i - jnp.einsum('bhlk,bhkv->bhlv', w_i, state)
        o = (jnp.einsum('bhlk,bhkv->bhlv', q_c * jnp.exp(g_c)[..., None], state)
             + jnp.einsum('bhlm,bhmv->bhlv', qk_c, v_new))
        g_last = g_c[..., -1]
        state = (state * jnp.exp(g_last)[..., None, None]
                 + jnp.einsum('bhlk,bhlv->bhkv', k_c * jnp.exp(g_last[..., None] - g_c)[..., None], v_new))
        return state, o

    bsz, nh = q.shape[:2]
    s0 = jnp.zeros((bsz, nh, dk, dv), jnp.float32)
    _, o = lax.scan(step, s0, xs)
    return from_chunks(o)


def gdn_mixer(u, conv_w, a_log, dt_bias, norm_g):
    bsz, seq, _ = u.shape
    nh = N_DIR * G_HEADS
    qkv, z, alpha, beta = jnp.split(u, [3 * G_WIDTH, 4 * G_WIDTH, 4 * G_WIDTH + nh], axis=-1)
    qkv = jax.nn.silu(conv_centred(qkv, conv_w))
    q, k, v = (split_heads(t) for t in jnp.split(qkv, 3, axis=-1))
    q, k = l2_normalize(q), l2_normalize(k)
    g = -jnp.exp(a_log) * jax.nn.softplus(alpha.reshape(bsz, seq, N_DIR, G_HEADS) + dt_bias)
    beta = jax.nn.sigmoid(beta.reshape(bsz, seq, N_DIR, G_HEADS))
    o_f = gated_delta_chunked(q, k, v, g[:, :, 0], beta[:, :, 0])
    o_b = flip_seq(gated_delta_chunked(flip_seq(q), flip_seq(k), flip_seq(v),
                                       flip_seq(g[:, :, 1]), flip_seq(beta[:, :, 1])))
    o = head_rms(o_f + o_b) * norm_g * jax.nn.silu(split_heads(z))
    return o.reshape(bsz, seq, G_WIDTH)


def mlstm_chunked(q, k, v, ig, lf):
    dk = q.shape[-1]
    dv = v.shape[-1]
    k = k * dk ** -0.5
    q, k, v = (to_chunks(t, CHUNK) for t in (q, k, v))
    ig, lf = (to_chunks(t, CHUNK) for t in (ig, lf))
    b = jnp.cumsum(lf, axis=-1)
    tri = jnp.tril(jnp.ones((CHUNK, CHUNK), bool))
    d_log = jnp.where(tri, b[..., :, None] - b[..., None, :] + ig[..., None, :], -jnp.inf)
    d_max = jnp.max(d_log, axis=-1)
    qk = jnp.einsum('bhcld,bhcmd->bhclm', q, k)
    w_end = b[..., -1:] - b + ig
    w_end_max = jnp.max(w_end, axis=-1)
    xs = tuple(jnp.moveaxis(t, 2, 0) for t in (q, k, v, b, d_log, d_max, qk, w_end, w_end_max))

    def step(carry, inp):
        c_st, n_st, m_st = carry
        q_c, k_c, v_c, b_c, dl_c, dm_c, qk_c, we_c, wem_c = inp
        inter = b_c + m_st[..., None]
        m_t = jnp.maximum(inter, dm_c)
        p = qk_c * jnp.exp(dl_c - m_t[..., None])
        s_inter = jnp.exp(inter - m_t)
        num = (jnp.einsum('bhlm,bhmd->bhld', p, v_c)
               + s_inter[..., None] * jnp.einsum('bhlk,bhkv->bhlv', q_c, c_st))
        den = jnp.sum(p, axis=-1) + s_inter * jnp.einsum('bhlk,bhk->bhl', q_c, n_st)
        h = num / jnp.maximum(jnp.abs(den), jnp.exp(-m_t))[..., None]
        m_new = jnp.maximum(b_c[..., -1] + m_st, wem_c)
        dec = jnp.exp(b_c[..., -1] + m_st - m_new)
        sc = jnp.exp(we_c - m_new[..., None])
        c_st = dec[..., None, None] * c_st + jnp.einsum('bhl,bhlk,bhlv->bhkv', sc, k_c, v_c)
        n_st = dec[..., None] * n_st + jnp.einsum('bhl,bhlk->bhk', sc, k_c)
        return (c_st, n_st, m_new), h

    bsz, nh = q.shape[:2]
    init = (jnp.zeros((bsz, nh, dk, dv), jnp.float32),
            jnp.zeros((bsz, nh, dk), jnp.float32),
            jnp.zeros((bsz, nh), jnp.float32))
    _, h = lax.scan(step, init, xs)
    return from_chunks(h)


def mlstm_mixer(u, gate_bias, norm_g):
    bsz, seq, _ = u.shape
    nh = N_DIR * G_HEADS
    q, k, v, o, gi, gf = jnp.split(
        u, [G_WIDTH, 2 * G_WIDTH, 3 * G_WIDTH, 4 * G_WIDTH, 4 * G_WIDTH + nh], axis=-1)
    q, k, v = split_heads(q), split_heads(k), split_heads(v)
    ig = soft_cap(gi.reshape(bsz, seq, N_DIR, G_HEADS) + gate_bias[:, 0])
    lf = jax.nn.log_sigmoid(soft_cap(gf.reshape(bsz, seq, N_DIR, G_HEADS) + gate_bias[:, 1]))
    h_f = mlstm_chunked(q, k, v, ig[:, :, 0], lf[:, :, 0])
    h_b = flip_seq(mlstm_chunked(flip_seq(q), flip_seq(k), flip_seq(v),
                                 flip_seq(ig[:, :, 1]), flip_seq(lf[:, :, 1])))
    h = head_rms(h_f + h_b) * split_heads(norm_g) * jax.nn.sigmoid(split_heads(o))
    return h.reshape(bsz, seq, G_WIDTH)


def rwkv_scan(r, w, k, v, kk, a, reverse):
    bsz, _, nh, d = r.shape

    def step(state, inp):
        r_t, w_t, k_t, v_t, kk_t, a_t = inp
        sa = jnp.einsum('bhij,bhj->bhi', state, -kk_t)
        state = (state * w_t[:, :, None, :] + sa[..., None] * (kk_t * a_t)[:, :, None, :]
                 + v_t[..., None] * k_t[:, :, None, :])
        return state, jnp.einsum('bhij,bhj->bhi', state, r_t)

    s0 = jnp.zeros((bsz, nh, d, d), jnp.float32)
    xs = tuple(jnp.moveaxis(t, 1, 0) for t in (r, w, k, v, kk, a))
    _, y = lax.scan(step, s0, xs, reverse=reverse)
    return jnp.moveaxis(y, 0, 1)


def rwkv_mixer(u, mu, w0, w_up, a0, a_up, g_up, k_k, k_a, r_k, gn_w, gn_b):
    bsz, seq, _ = u.shape
    u_prev = jnp.pad(u, ((0, 0), (1, 0), (0, 0)))[:, :-1]
    u_next = jnp.pad(u, ((0, 0), (0, 1), (0, 0)))[:, 1:]
    u = u + mu[0] * (u_prev - u) + mu[1] * (u_next - u)
    c1 = 3 * G_WIDTH + N_DIR * RWKV_W_RANK
    c2 = c1 + N_DIR * RWKV_A_RANK
    r, k, v, wd, ad, gd = jnp.split(u, [G_WIDTH, 2 * G_WIDTH, 3 * G_WIDTH, c1, c2], axis=-1)
    wd = wd.reshape(bsz, seq, N_DIR, RWKV_W_RANK)
    ad = ad.reshape(bsz, seq, N_DIR, RWKV_A_RANK)
    z_w = w0 + jnp.einsum('bsnr,nrc->bsnc', jnp.tanh(wd), w_up)
    w = jnp.exp(-jnp.exp(-jax.nn.softplus(-z_w) - 0.5))
    a = jax.nn.sigmoid(a0 + jnp.einsum('bsnr,nrc->bsnc', ad, a_up))
    gate = jnp.einsum('bsr,rc->bsc', jax.nn.sigmoid(gd), g_up)
    kk = l2_normalize(split_heads(k * k_k))
    k_dir = k[:, :, None, :] * (1.0 + (a - 1.0) * k_a)
    rh, vh = split_heads(r), split_heads(v)
    y_f = rwkv_scan(rh, split_heads(w[:, :, 0]), split_heads(k_dir[:, :, 0]), vh, kk,
                    split_heads(a[:, :, 0]), False)
    y_b = rwkv_scan(rh, split_heads(w[:, :, 1]), split_heads(k_dir[:, :, 1]), vh, kk,
                    split_heads(a[:, :, 1]), True)
    y = y_f + y_b
    mean = jnp.mean(y, axis=-1, keepdims=True)
    var = jnp.mean(jnp.square(y - mean), axis=-1, keepdims=True)
    y = (y - mean) * lax.rsqrt(var + RWKV_GN_EPS) * split_heads(gn_w) + split_heads(gn_b)
    bonus = jnp.sum(rh * split_heads(k) * r_k, axis=-1, keepdims=True) * vh
    return (y + bonus).reshape(bsz, seq, G_WIDTH) * gate


def hybrid_layer(x, n_mix_pre, n_mix_post, n_ffn_pre, n_ffn_post, w_in, w_out,
                 lru_conv_w, lru_conv_b, lru_gate_w, lru_gate_b, lru_lambda,
                 gdn_conv_w, gdn_a_log, gdn_dt_bias, gdn_norm,
                 mlstm_gate_bias, mlstm_norm,
                 rwkv_mu, rwkv_w0, rwkv_w_up, rwkv_a0, rwkv_a_up, rwkv_g_up,
                 rwkv_k_k, rwkv_k_a, rwkv_r_k, rwkv_gn_w, rwkv_gn_b,
                 ffn_w_in, ffn_w_out):
    h = rms_norm(x, n_mix_pre)
    u = jnp.einsum('bsd,dp->bsp', h, w_in).astype(jnp.float32)
    u_a, u_b, u_c, u_d = jnp.split(u, [A_COLS, A_COLS + B_COLS, A_COLS + B_COLS + C_COLS], axis=-1)
    y = jnp.concatenate([
        rglru_mixer(u_a, lru_conv_w, lru_conv_b, lru_gate_w, lru_gate_b, lru_lambda),
        gdn_mixer(u_b, gdn_conv_w, gdn_a_log, gdn_dt_bias, gdn_norm),
        mlstm_mixer(u_c, mlstm_gate_bias, mlstm_norm),
        rwkv_mixer(u_d, rwkv_mu, rwkv_w0, rwkv_w_up, rwkv_a0, rwkv_a_up, rwkv_g_up,
                   rwkv_k_k, rwkv_k_a, rwkv_r_k, rwkv_gn_w, rwkv_gn_b),
    ], axis=-1).astype(x.dtype)
    x = x + rms_norm(jnp.einsum('bsm,md->bsd', y, w_out), n_mix_post)
    h = rms_norm(x, n_ffn_pre)
    gate, up = jnp.split(jnp.einsum('bsd,df->bsf', h, ffn_w_in), 2, axis=-1)
    ff = jnp.einsum('bsf,fd->bsd', jax.nn.silu(gate) * up, ffn_w_out)
    return x + rms_norm(ff, n_ffn_post)


def setup_inputs(seed: int = 0) -> dict:
    key = jax.random.key(seed)
    ks = iter(jax.random.split(key, 40))
    L = DEPTH

    def nrm(shape, scale):
        return scale * jax.random.normal(next(ks), shape, jnp.float32)

    def uni(shape, lo, hi):
        return jax.random.uniform(next(ks), shape, jnp.float32, lo, hi)

    def gain(shape):
        return 1.0 + nrm(shape, 0.02)

    x = nrm((BATCH, SEQ, D_MODEL), 1.0)
    a_init = uni((L, N_DIR, G_WIDTH), 0.9, 0.999)
    dt = jnp.exp(uni((L, N_DIR, G_HEADS), math.log(1e-3), math.log(1e-1)))
    mlstm_gate_bias = jnp.stack([nrm((L, N_DIR, G_HEADS), 0.1),
                                 uni((L, N_DIR, G_HEADS), 3.0, 6.0)], axis=2)
    bw = G_WIDTH // LRU_BLOCKS
    return {
        'x': x,
        'norm_mix_pre': gain((L, D_MODEL)),
        'norm_mix_post': gain((L, D_MODEL)),
        'norm_ffn_pre': gain((L, D_MODEL)),
        'norm_ffn_post': gain((L, D_MODEL)),
        'w_in': nrm((L, D_MODEL, P_IN), D_MODEL ** -0.5),
        'w_out': nrm((L, D_MIX, D_MODEL), D_MIX ** -0.5),
        'lru_conv_w': nrm((L, CONV_K, G_WIDTH), CONV_K ** -0.5),
        'lru_conv_b': nrm((L, G_WIDTH), 0.02),
        'lru_gate_w': nrm((L, N_DIR, 2, LRU_BLOCKS, bw, bw), bw ** -0.5),
        'lru_gate_b': nrm((L, N_DIR, 2, G_WIDTH), 0.02),
        'lru_lambda': jnp.log(a_init) - jnp.log1p(-a_init),
        'gdn_conv_w': nrm((L, CONV_K, 3 * G_WIDTH), CONV_K ** -0.5),
        'gdn_a_log': jnp.log(uni((L, N_DIR, G_HEADS), 1.0, 16.0)),
        'gdn_dt_bias': dt + jnp.log(-jnp.expm1(-dt)),
        'gdn_norm': gain((L, HEAD_DIM)),
        'mlstm_gate_bias': mlstm_gate_bias,
        'mlstm_norm': gain((L, G_WIDTH)),
        'rwkv_mu': uni((L, 2, D_COLS), 0.0, 0.5),
        'rwkv_w0': uni((L, N_DIR, G_WIDTH), -6.0, 1.0),
        'rwkv_w_up': nrm((L, N_DIR, RWKV_W_RANK, G_WIDTH), 0.1 * RWKV_W_RANK ** -0.5),
        'rwkv_a0': nrm((L, N_DIR, G_WIDTH), 0.1),
        'rwkv_a_up': nrm((L, N_DIR, RWKV_A_RANK, G_WIDTH), 0.1 * RWKV_A_RANK ** -0.5),
        'rwkv_g_up': nrm((L, RWKV_G_RANK, G_WIDTH), RWKV_G_RANK ** -0.5),
        'rwkv_k_k': 0.85 + nrm((L, G_WIDTH), 0.02),
        'rwkv_k_a': gain((L, G_WIDTH)),
        'rwkv_r_k': nrm((L, G_HEADS, HEAD_DIM), 0.1),
        'rwkv_gn_w': gain((L, G_WIDTH)),
        'rwkv_gn_b': nrm((L, G_WIDTH), 0.02),
        'ffn_w_in': nrm((L, D_MODEL, 2 * D_FF), D_MODEL ** -0.5),
        'ffn_w_out': nrm((L, D_FF, D_MODEL), D_FF ** -0.5),
    }


def reference(x, norm_mix_pre, norm_mix_post, norm_ffn_pre, norm_ffn_post, w_in, w_out,
              lru_conv_w, lru_conv_b, lru_gate_w, lru_gate_b, lru_lambda,
              gdn_conv_w, gdn_a_log, gdn_dt_bias, gdn_norm,
              mlstm_gate_bias, mlstm_norm,
              rwkv_mu, rwkv_w0, rwkv_w_up, rwkv_a0, rwkv_a_up, rwkv_g_up,
              rwkv_k_k, rwkv_k_a, rwkv_r_k, rwkv_gn_w, rwkv_gn_b,
              ffn_w_in, ffn_w_out):
    for l in range(DEPTH):
        x = hybrid_layer(
            x, norm_mix_pre[l], norm_mix_post[l], norm_ffn_pre[l], norm_ffn_post[l], w_in[l], w_out[l],
            lru_conv_w[l], lru_conv_b[l], lru_gate_w[l], lru_gate_b[l], lru_lambda[l],
            gdn_conv_w[l], gdn_a_log[l], gdn_dt_bias[l], gdn_norm[l],
            mlstm_gate_bias[l], mlstm_norm[l],
            rwkv_mu[l], rwkv_w0[l], rwkv_w_up[l], rwkv_a0[l], rwkv_a_up[l], rwkv_g_up[l],
            rwkv_k_k[l], rwkv_k_a[l], rwkv_r_k[l], rwkv_gn_w[l], rwkv_gn_b[l],
            ffn_w_in[l], ffn_w_out[l])
    return x
```

```python
import functools

import jax
import jax.numpy as jnp
from jax import lax
from jax.experimental import pallas as pl
from jax.experimental.pallas import tpu as pltpu

F32 = jnp.float32
BF16 = jnp.bfloat16

HEAD_DIM = 64
N_HEADS = 4
G_WIDTH = N_HEADS * HEAD_DIM
N_DIR = 2
CONV_K = 4
CHUNK = 64
CHUNK_SHIFT = 6
LRU_C = 8.0
GATE_CAP = 15.0
RWKV_RANK_COLS = 128
RWKV_G_RANK = 64
RWKV_GN_EPS = 64e-5
RMS_EPS = 1e-6
L2_EPS = 1e-6

LANES = 128
HALO = 8
T_BLK = 256
TM_PROJ = 512
VMEM_LIMIT = 56 * 1024 * 1024

OFF_GDN = 0
OFF_MLSTM = 4 * G_WIDTH
OFF_RWKV = 8 * G_WIDTH
OFF_LRU = 12 * G_WIDTH
P_PAD = 14 * G_WIDTH
GATE_TILE_BLK = (OFF_LRU - LANES) // LANES
LANE_ALPHA = RWKV_G_RANK
LANE_BETA = LANE_ALPHA + N_DIR * N_HEADS
LANE_GI = LANE_BETA + N_DIR * N_HEADS
LANE_GF = LANE_GI + N_DIR * N_HEADS


def _bdot(a, b):
    return jnp.dot(a.astype(BF16), b.astype(BF16), preferred_element_type=F32)


def _bdot_nt(a, b):
    return lax.dot_general(a.astype(BF16), b.astype(BF16), (((1,), (1,)), ((), ())),
                           preferred_element_type=F32)


def _bdot_tn(a, b):
    return lax.dot_general(a.astype(BF16), b.astype(BF16), (((0,), (0,)), ((), ())),
                           preferred_element_type=F32)


def _split_bf16(x, n):
    parts, r = [], x
    for _ in range(n):
        p = r.astype(BF16)
        parts.append(p)
        r = r - p.astype(F32)
    return parts


def _dot_sel_lhs(m_bf16, x, n=3):
    acc = None
    for p in _split_bf16(x, n):
        t = jnp.dot(m_bf16, p, preferred_element_type=F32)
        acc = t if acc is None else acc + t
    return acc


def _dot_sel_rhs(x, m_bf16, n=2):
    acc = None
    for p in _split_bf16(x, n):
        t = jnp.dot(p, m_bf16, preferred_element_type=F32)
        acc = t if acc is None else acc + t
    return acc


def _softplus(x):
    return jnp.maximum(x, 0.0) + jnp.log1p(jnp.exp(-jnp.abs(x)))


def _sigmoid(x):
    return 1.0 / (1.0 + jnp.exp(-x))


def _silu(x):
    return x * _sigmoid(x)


def _gelu_tanh(x):
    c = 0.7978845608028654
    return x * (0.5 * (1.0 + jnp.tanh(c * (x + 0.044715 * (x * x * x)))))


def _soft_cap(x):
    return GATE_CAP * jnp.tanh(x / GATE_CAP)


def _rms(x, g):
    return x * lax.rsqrt(jnp.mean(x * x, axis=-1, keepdims=True) + RMS_EPS) * g


def _head_sum_matrix(width):
    i = lax.broadcasted_iota(jnp.int32, (width, width), 0)
    j = lax.broadcasted_iota(jnp.int32, (width, width), 1)
    return jnp.where((i >> CHUNK_SHIFT) == (j >> CHUNK_SHIFT), 1.0, 0.0).astype(BF16)


def _chunk_masks(t, rev):
    i = lax.broadcasted_iota(jnp.int32, (t, t), 0)
    j = lax.broadcasted_iota(jnp.int32, (t, t), 1)
    same = (i >> CHUNK_SHIFT) == (j >> CHUNK_SHIFT)
    if rev:
        return same & (i <= j), same & (i < j)
    return same & (i >= j), same & (i > j)


def _solve_nilpotent(m, rhs):
    x = rhs + _bdot(m, rhs)
    p = m
    for _ in range(CHUNK_SHIFT - 1):
        p = _bdot(p, p)
        x = x + _bdot(p, x)
    return x


def _chunk_order(t, rev):
    n = t // CHUNK
    return list(range(n - 1, -1, -1)) if rev else list(range(n))


def _fill_halo(xe_ref, hp_ref, hn_ref, body, blk, nt, t):
    xe_ref[0:HALO, :] = jnp.where(blk == 0, 0.0, hp_ref[0])
    xe_ref[HALO:HALO + t, :] = body
    xe_ref[HALO + t:HALO + t + HALO, :] = jnp.where(blk == nt - 1, 0.0, hn_ref[0])


def _inproj_kernel(x_ref, g_ref, w_ref, u_ref):
    h = _rms(x_ref[...], g_ref[...])
    u_ref[...] = jnp.dot(h.astype(BF16), w_ref[...], preferred_element_type=F32)


def _inproj(x2, g, w):
    n, d = x2.shape
    p = w.shape[1]
    return pl.pallas_call(
        _inproj_kernel,
        out_shape=jax.ShapeDtypeStruct((n, p), F32),
        grid=(n // TM_PROJ,),
        in_specs=[pl.BlockSpec((TM_PROJ, d), lambda i: (i, 0)),
                  pl.BlockSpec((1, d), lambda i: (0, 0)),
                  pl.BlockSpec((d, p), lambda i: (0, 0))],
        out_specs=pl.BlockSpec((TM_PROJ, p), lambda i: (i, 0)),
        compiler_params=pltpu.CompilerParams(dimension_semantics=("arbitrary",),
                                             vmem_limit_bytes=VMEM_LIMIT),
        name="in_proj",
    )(x2, g, w)


def _lru_kernel(rev, nt, u_ref, hp_ref, hn_ref, cw_ref, cb_ref, wg_ref, gb_ref, lam_ref, *rest):
    if rev:
        hf_ref, out_ref, xe_ref, a_ref, b_ref, h_ref, carry_ref = rest
    else:
        out_ref, xe_ref, a_ref, b_ref, h_ref, carry_ref = rest
    t = T_BLK
    i = pl.program_id(1)
    blk = nt - 1 - i if rev else i
    x = u_ref[0]
    xb = x[:, :G_WIDTH]
    _fill_halo(xe_ref, hp_ref, hn_ref, xb, blk, nt, t)
    xc = cb_ref[...]
    for j in range(CONV_K):
        xc = xc + cw_ref[j:j + 1, :] * xe_ref[pl.ds(HALO - CONV_K // 2 + j, t), :]
    pre = _bdot(xc, wg_ref[...]) + gb_ref[...]
    r = _sigmoid(pre[:, :G_WIDTH])
    ig = _sigmoid(pre[:, G_WIDTH:])
    log_a = -LRU_C * r * _softplus(-lam_ref[...])
    a_ref[...] = jnp.exp(log_a)
    th = jnp.tanh(log_a)
    b_ref[...] = jnp.sqrt(-2.0 * th / (1.0 - th)) * ig * xc

    @pl.when(i == 0)
    def _():
        carry_ref[...] = jnp.zeros_like(carry_ref)

    def step(s, h):
        row = t - 1 - s if rev else s
        h = a_ref[pl.ds(row, 1), :] * h + b_ref[pl.ds(row, 1), :]
        h_ref[pl.ds(row, 1), :] = h
        return h

    carry_ref[...] = lax.fori_loop(0, t, step, carry_ref[...], unroll=8)
    if rev:
        out_ref[0] = (hf_ref[0] + h_ref[...]) * _gelu_tanh(x[:, G_WIDTH:])
    else:
        out_ref[0] = h_ref[...]


def _gdn_kernel(rev, nt, u_ref, hp_ref, hn_ref, gt_ref, cw_ref, alog_ref, dtb_ref, ng_ref, *rest):
    if rev:
        of_ref, out_ref, xe_ref, s_ref, o_ref = rest
    else:
        out_ref, xe_ref, s_ref, o_ref = rest
    t = T_BLK
    d = 1 if rev else 0
    i = pl.program_id(1)
    blk = nt - 1 - i if rev else i

    @pl.when(i == 0)
    def _():
        s_ref[...] = jnp.zeros_like(s_ref)

    x = u_ref[0]
    _fill_halo(xe_ref, hp_ref, hn_ref, x[:, :3 * G_WIDTH], blk, nt, t)
    acc = None
    for j in range(CONV_K):
        term = cw_ref[j:j + 1, :] * xe_ref[pl.ds(HALO - CONV_K // 2 + j, t), :]
        acc = term if acc is None else acc + term
    qkv = _silu(acc)
    hsum = _head_sum_matrix(G_WIDTH)
    q = qkv[:, :G_WIDTH]
    k = qkv[:, G_WIDTH:2 * G_WIDTH]
    v = qkv[:, 2 * G_WIDTH:]
    q = q * lax.rsqrt(_dot_sel_rhs(q * q, hsum) + L2_EPS) * (HEAD_DIM ** -0.5)
    k = k * lax.rsqrt(_dot_sel_rhs(k * k, hsum) + L2_EPS)

    gt = gt_ref[0]
    g_all = -jnp.exp(alog_ref[...]) * _softplus(gt + dtb_ref[...])
    beta_all = _sigmoid(gt)
    tri, strict = _chunk_masks(t, rev)
    tri_bf = jnp.where(tri, 1.0, 0.0).astype(BF16)
    gc_all = _dot_sel_lhs(tri_bf, g_all)
    gc_rows = gc_all.T

    for h in range(N_HEADS):
        la = LANE_ALPHA + N_HEADS * d + h
        lb = LANE_BETA + N_HEADS * d + h
        hs = slice(HEAD_DIM * h, HEAD_DIM * (h + 1))
        gc_c = gc_all[:, la:la + 1]
        gc_r = gc_rows[la:la + 1, :]
        beta = beta_all[:, lb:lb + 1]
        dec = jnp.exp(jnp.where(tri, gc_c - gc_r, -jnp.inf))
        qh, kh, vh = q[:, hs], k[:, hs], v[:, hs]
        kb = kh * beta
        m = jnp.where(strict, -(_bdot_nt(kb, kh) * dec), 0.0)
        sol = _solve_nilpotent(m, jnp.concatenate([vh * beta, kb * jnp.exp(gc_c)], axis=1))
        uc, wc = sol[:, :HEAD_DIM], sol[:, HEAD_DIM:]
        qk = _bdot_nt(qh, kh) * dec
        qe = qh * jnp.exp(gc_c)
        s = s_ref[h]
        for c in _chunk_order(t, rev):
            sl = slice(CHUNK * c, CHUNK * (c + 1))
            rl = CHUNK * c if rev else CHUNK * (c + 1) - 1
            g_last = gc_c[rl:rl + 1, :]
            ws = _bdot(jnp.concatenate([wc[sl], qe[sl]], axis=0), s)
            v_new = uc[sl] - ws[:CHUNK]
            o_ref[sl, hs] = ws[CHUNK:] + _bdot(qk[sl, sl], v_new)
            kd = kh[sl] * jnp.exp(g_last - gc_c[sl])
            s = s * jnp.exp(g_last) + _bdot_tn(kd, v_new)
        s_ref[h] = s

    if rev:
        o = of_ref[0] + o_ref[...]
        ms = _dot_sel_rhs(o * o, hsum) * (1.0 / HEAD_DIM)
        out_ref[0] = o * lax.rsqrt(ms + RMS_EPS) * ng_ref[...] * _silu(x[:, 3 * G_WIDTH:])
    else:
        out_ref[0] = o_ref[...]


def _mlstm_kernel(rev, nt, u_ref, gt_ref, bias_ref, ng_ref, *rest):
    if rev:
        hf_ref, out_ref, c_ref, m_ref, h_ref = rest
    else:
        out_ref, c_ref, m_ref, h_ref = rest
    t = T_BLK
    d = 1 if rev else 0
    i = pl.program_id(1)

    @pl.when(i == 0)
    def _():
        c_ref[...] = jnp.zeros_like(c_ref)
        m_ref[...] = jnp.zeros_like(m_ref)

    x = u_ref[0]
    q = x[:, :G_WIDTH]
    k = x[:, G_WIDTH:2 * G_WIDTH] * (HEAD_DIM ** -0.5)
    v = x[:, 2 * G_WIDTH:3 * G_WIDTH]

    pre = _soft_cap(gt_ref[0] + bias_ref[...])
    ig_all = pre
    lf_all = -_softplus(-pre)
    tri, _ = _chunk_masks(t, rev)
    tri_bf = jnp.where(tri, 1.0, 0.0).astype(BF16)
    b_all = _dot_sel_lhs(tri_bf, lf_all)
    b_rows = b_all.T
    ig_rows = ig_all.T
    one_col = jnp.where(lax.broadcasted_iota(jnp.int32, (t, HEAD_DIM), 1) == 0, 1.0, 0.0)

    for h in range(N_HEADS):
        li = LANE_GI + N_HEADS * d + h
        lf = LANE_GF + N_HEADS * d + h
        hs = slice(HEAD_DIM * h, HEAD_DIM * (h + 1))
        b_c = b_all[:, lf:lf + 1]
        ig_c = ig_all[:, li:li + 1]
        d_log = jnp.where(tri, b_c - b_rows[lf:lf + 1, :] + ig_rows[li:li + 1, :], -jnp.inf)
        d_max = jnp.max(d_log, axis=1, keepdims=True)

        m_run = m_ref[h][0:1, 0:1]
        per_chunk = {}
        for c in _chunk_order(t, rev):
            sl = slice(CHUNK * c, CHUNK * (c + 1))
            rl = CHUNK * c if rev else CHUNK * (c + 1) - 1
            b_last = b_c[rl:rl + 1, :]
            w_end = b_last - b_c[sl] + ig_c[sl]
            m_new = jnp.maximum(b_last + m_run, jnp.max(w_end, axis=0, keepdims=True))
            per_chunk[c] = (m_run, m_new, w_end, b_last)
            m_run = m_new
        m_ref[h] = jnp.broadcast_to(m_run, m_ref.shape[1:])

        inter = jnp.concatenate(
            [b_c[CHUNK * c:CHUNK * (c + 1)] + per_chunk[c][0] for c in range(t // CHUNK)], axis=0)
        m_t = jnp.maximum(inter, d_max)
        qh, kh = q[:, hs], k[:, hs]
        v_ext = jnp.concatenate([v[:, hs], one_col], axis=1)
        p = _bdot_nt(qh, kh) * jnp.exp(d_log - m_t)
        s_inter = jnp.exp(inter - m_t)
        intra = _bdot(p, v_ext)
        floor = jnp.exp(-m_t)
        cn = c_ref[h]
        for c in _chunk_order(t, rev):
            sl = slice(CHUNK * c, CHUNK * (c + 1))
            m_old, m_new, w_end, b_last = per_chunk[c]
            nd = intra[sl] + s_inter[sl] * _bdot(qh[sl], cn)
            den = jnp.maximum(jnp.abs(nd[:, HEAD_DIM:HEAD_DIM + 1]), floor[sl])
            h_ref[sl, hs] = nd[:, :HEAD_DIM] / den
            dec = jnp.exp(b_last + m_old - m_new)
            sc = jnp.exp(w_end - m_new)
            cn = dec * cn + _bdot_tn(sc * kh[sl], v_ext[sl])
        c_ref[h] = cn

    if rev:
        hh = hf_ref[0] + h_ref[...]
        ms = _dot_sel_rhs(hh * hh, _head_sum_matrix(G_WIDTH)) * (1.0 / HEAD_DIM)
        out_ref[0] = hh * lax.rsqrt(ms + RMS_EPS) * ng_ref[...] * _sigmoid(x[:, 3 * G_WIDTH:])
    else:
        out_ref[0] = h_ref[...]


def _rwkv_kernel(rev, nt, u_ref, hp_ref, hn_ref, mu_ref, wup_ref, w0_ref, a0_ref, gup_ref,
                 kk_ref, ka_ref, rk_ref, gnw_ref, gnb_ref, *rest):
    if rev:
        yf_ref, out_ref, xe_ref, s_ref, y_ref = rest
    else:
        out_ref, xe_ref, s_ref, y_ref = rest
    t = T_BLK
    i = pl.program_id(1)
    blk = nt - 1 - i if rev else i

    @pl.when(i == 0)
    def _():
        s_ref[...] = jnp.zeros_like(s_ref)

    x = u_ref[0]
    _fill_halo(xe_ref, hp_ref, hn_ref, x, blk, nt, t)
    x_prev = xe_ref[pl.ds(HALO - 1, t), :]
    x_next = xe_ref[pl.ds(HALO + 1, t), :]
    us = x + mu_ref[0:1, :] * (x_prev - x) + mu_ref[1:2, :] * (x_next - x)
    r = us[:, :G_WIDTH]
    k = us[:, G_WIDTH:2 * G_WIDTH]
    v = us[:, 2 * G_WIDTH:3 * G_WIDTH]
    low = us[:, 3 * G_WIDTH:3 * G_WIDTH + RWKV_RANK_COLS]
    lane = lax.broadcasted_iota(jnp.int32, (t, RWKV_RANK_COLS), 1)
    low = jnp.where(lane < RWKV_RANK_COLS // 2, jnp.tanh(low), low)
    za = _bdot(low, wup_ref[...])
    z_w = w0_ref[...] + za[:, :G_WIDTH]
    lw = -jnp.exp(-_softplus(-z_w) - 0.5)
    a = _sigmoid(a0_ref[...] + za[:, G_WIDTH:])
    gate = _bdot(_sigmoid(us[:, 3 * G_WIDTH + RWKV_RANK_COLS:]), gup_ref[...])
    hsum = _head_sum_matrix(G_WIDTH)
    kk = k * kk_ref[...]
    kk = kk * lax.rsqrt(_dot_sel_rhs(kk * kk, hsum) + L2_EPS)
    k_dir = k * (1.0 + (a - 1.0) * ka_ref[...])
    akk = a * kk

    tri, strict = _chunk_masks(t, rev)
    tri_bf = jnp.where(tri, 1.0, 0.0).astype(BF16)
    gc = _dot_sel_lhs(tri_bf, lw)
    e_inc = jnp.exp(gc)
    e_exc = jnp.exp(gc - lw)
    e_neg = jnp.exp(-gc)
    left_a = -kk * e_exc
    left_r = r * e_inc
    right_b = akk * e_neg
    right_k = k_dir * e_neg

    for h in range(N_HEADS):
        hs = slice(HEAD_DIM * h, HEAD_DIM * (h + 1))
        la_h, lr_h, vh, gc_h = left_a[:, hs], left_r[:, hs], v[:, hs], gc[:, hs]
        gram = _bdot_nt(jnp.concatenate([la_h, lr_h], axis=0),
                        jnp.concatenate([right_b[:, hs], right_k[:, hs]], axis=0))
        a_ab = jnp.where(strict, gram[:t, :t], 0.0)
        a_ak = jnp.where(strict, gram[:t, t:], 0.0)
        a_rb = jnp.where(tri, gram[t:, :t], 0.0)
        a_rk = jnp.where(tri, gram[t:, t:], 0.0)
        sol = _solve_nilpotent(a_ab, jnp.concatenate([la_h, _bdot(a_ak, vh)], axis=1))
        wm, um = sol[:, :HEAD_DIM], sol[:, HEAD_DIM:]
        yv = _bdot(a_rk, vh)
        st = s_ref[h]
        for c in _chunk_order(t, rev):
            sl = slice(CHUNK * c, CHUNK * (c + 1))
            rl = CHUNK * c if rev else CHUNK * (c + 1) - 1
            both = _bdot_nt(jnp.concatenate([wm[sl], lr_h[sl]], axis=0), st)
            p = both[:CHUNK] + um[sl]
            y_ref[sl, hs] = both[CHUNK:] + _bdot(a_rb[sl, sl], p) + yv[sl]
            g_last = gc_h[rl:rl + 1, :]
            tail = jnp.exp(g_last - gc_h[sl])
            st = st * jnp.exp(g_last) + _bdot_tn(
                jnp.concatenate([p, vh[sl]], axis=0),
                jnp.concatenate([akk[sl, hs] * tail, k_dir[sl, hs] * tail], axis=0))
        s_ref[h] = st

    if rev:
        y = yf_ref[0] + y_ref[...]
        mean = _dot_sel_rhs(y, hsum) * (1.0 / HEAD_DIM)
        dlt = y - mean
        var = _dot_sel_rhs(dlt * dlt, hsum) * (1.0 / HEAD_DIM)
        yn = dlt * lax.rsqrt(var + RWKV_GN_EPS) * gnw_ref[...] + gnb_ref[...]
        bonus = _dot_sel_rhs(r * k * rk_ref[...], hsum) * v
        out_ref[0] = (yn + bonus) * gate
    else:
        out_ref[0] = y_ref[...]


def _mixer_call(body, rev, u3, col_blk, col_w, halo_w, use_gates, params, prev_out, scratch, name):
    bsz, seq, _ = u3.shape
    t = T_BLK
    nt = seq // t
    rows = t // HALO

    def blk_of(i):
        return nt - 1 - i if rev else i

    in_specs = [pl.BlockSpec((1, t, col_w), lambda b, i: (b, blk_of(i), col_blk))]
    args = [u3]
    if halo_w:
        halo_blk = col_blk * col_w // halo_w
        in_specs += [
            pl.BlockSpec((1, HALO, halo_w),
                         lambda b, i: (b, jnp.maximum(blk_of(i) * rows - 1, 0), halo_blk)),
            pl.BlockSpec((1, HALO, halo_w),
                         lambda b, i: (b, jnp.minimum((blk_of(i) + 1) * rows, seq // HALO - 1), halo_blk)),
        ]
        args += [u3, u3]
    if use_gates:
        in_specs.append(pl.BlockSpec((1, t, LANES), lambda b, i: (b, blk_of(i), GATE_TILE_BLK)))
        args.append(u3)
    for p in params:
        in_specs.append(pl.BlockSpec(p.shape, lambda b, i, nd=p.ndim: (0,) * nd))
        args.append(p)
    if rev:
        in_specs.append(pl.BlockSpec((1, t, G_WIDTH), lambda b, i: (b, blk_of(i), 0)))
        args.append(prev_out)
    return pl.pallas_call(
        functools.partial(body, rev, nt),
        out_shape=jax.ShapeDtypeStruct((bsz, seq, G_WIDTH), F32),
        grid=(bsz, nt),
        in_specs=in_specs,
        out_specs=pl.BlockSpec((1, t, G_WIDTH), lambda b, i: (b, blk_of(i), 0)),
        scratch_shapes=scratch,
        compiler_params=pltpu.CompilerParams(dimension_semantics=("arbitrary", "arbitrary"),
                                             vmem_limit_bytes=VMEM_LIMIT),
        name=name + ("_bwd" if rev else "_fwd"),
    )(*args)


def _both_dirs(body, u3, col_blk, col_w, halo_w, use_gates, params_of_dir, scratch, name):
    fwd = _mixer_call(body, False, u3, col_blk, col_w, halo_w, use_gates, params_of_dir(0), None,
                      scratch, name)
    return _mixer_call(body, True, u3, col_blk, col_w, halo_w, use_gates, params_of_dir(1), fwd,
                       scratch, name)


def _row(v):
    return v.reshape(1, -1).astype(F32)


def _gate_row(values, lane0):
    flat = values.reshape(-1).astype(F32)
    return jnp.zeros((1, LANES), F32).at[0, lane0:lane0 + flat.shape[0]].set(flat)


def _lru_mixer(u3, conv_w, conv_b, gate_w, gate_b, lam):
    t = T_BLK
    eye = jnp.eye(N_HEADS, dtype=F32)

    def params(d):
        dense = jnp.einsum('gnio,nm->nigmo', gate_w[d], eye).reshape(G_WIDTH, 2 * G_WIDTH)
        return [conv_w, _row(conv_b), dense.astype(BF16), _row(gate_b[d]), _row(lam[d])]

    scratch = [pltpu.VMEM((t + 2 * HALO, G_WIDTH), F32), pltpu.VMEM((t, G_WIDTH), F32),
               pltpu.VMEM((t, G_WIDTH), F32), pltpu.VMEM((t, G_WIDTH), F32),
               pltpu.VMEM((1, G_WIDTH), F32)]
    return _both_dirs(_lru_kernel, u3, OFF_LRU // (2 * G_WIDTH), 2 * G_WIDTH, G_WIDTH, False,
                      params, scratch, "rglru")


def _gdn_mixer(u3, conv_w, a_log, dt_bias, norm_g):
    t = T_BLK
    shared = [conv_w, _gate_row(a_log, LANE_ALPHA), _gate_row(dt_bias, LANE_ALPHA),
              _row(jnp.tile(norm_g, N_HEADS))]
    scratch = [pltpu.VMEM((t + 2 * HALO, 3 * G_WIDTH), F32),
               pltpu.VMEM((N_HEADS, HEAD_DIM, HEAD_DIM), F32), pltpu.VMEM((t, G_WIDTH), F32)]
    return _both_dirs(_gdn_kernel, u3, OFF_GDN // (4 * G_WIDTH), 4 * G_WIDTH, 3 * G_WIDTH, True,
                      lambda d: shared, scratch, "gdn")


def _mlstm_mixer(u3, gate_bias, norm_g):
    t = T_BLK
    bias = (_gate_row(gate_bias[:, 0], LANE_GI) + _gate_row(gate_bias[:, 1], LANE_GF))
    shared = [bias, _row(norm_g)]
    scratch = [pltpu.VMEM((N_HEADS, HEAD_DIM, 2 * HEAD_DIM), F32),
               pltpu.VMEM((N_HEADS, HALO, LANES), F32), pltpu.VMEM((t, G_WIDTH), F32)]
    return _both_dirs(_mlstm_kernel, u3, OFF_MLSTM // (4 * G_WIDTH), 4 * G_WIDTH, 0, True,
                      lambda d: shared, scratch, "mlstm")


def _rwkv_mixer(u3, mu, w0, w_up, a0, a_up, g_up, k_k, k_a, r_k, gn_w, gn_b):
    t = T_BLK
    rank = w_up.shape[1]
    mu_pad = jnp.pad(mu, ((0, 0), (0, 4 * G_WIDTH - mu.shape[1])))
    gup_pad = jnp.pad(g_up, ((0, LANES - g_up.shape[0]), (0, 0))).astype(BF16)

    def params(d):
        wup = jnp.zeros((RWKV_RANK_COLS, 2 * G_WIDTH), F32)
        wup = wup.at[rank * d:rank * (d + 1), :G_WIDTH].set(w_up[d])
        half = RWKV_RANK_COLS // 2
        wup = wup.at[half + rank * d:half + rank * (d + 1), G_WIDTH:].set(a_up[d])
        return [mu_pad, wup.astype(BF16), _row(w0[d]), _row(a0[d]), gup_pad, _row(k_k), _row(k_a),
                _row(r_k), _row(gn_w), _row(gn_b)]

    scratch = [pltpu.VMEM((t + 2 * HALO, 4 * G_WIDTH), F32),
               pltpu.VMEM((N_HEADS, HEAD_DIM, HEAD_DIM), F32), pltpu.VMEM((t, G_WIDTH), F32)]
    return _both_dirs(_rwkv_kernel, u3, OFF_RWKV // (4 * G_WIDTH), 4 * G_WIDTH, 4 * G_WIDTH, False,
                      params, scratch, "rwkv")


def _outffn_kernel(n_ff_chunks, x_ref, ya_ref, yb_ref, yc_ref, yd_ref, wout_ref, gpost_ref, gpre_ref,
                   wfi_ref, wfo_ref, gfpost_ref, o_ref):
    y = jnp.concatenate([ya_ref[...], yb_ref[...], yc_ref[...], yd_ref[...]], axis=-1)
    mixed = jnp.dot(y.astype(BF16), wout_ref[...], preferred_element_type=F32)
    x1 = x_ref[...] + _rms(mixed, gpost_ref[...])
    h = _rms(x1, gpre_ref[...]).astype(BF16)
    d_ff = wfo_ref.shape[0]
    fc = d_ff // n_ff_chunks
    acc = None
    for j in range(n_ff_chunks):
        gate = jnp.dot(h, wfi_ref[:, fc * j:fc * (j + 1)], preferred_element_type=F32)
        up = jnp.dot(h, wfi_ref[:, d_ff + fc * j:d_ff + fc * (j + 1)], preferred_element_type=F32)
        part = jnp.dot((_silu(gate) * up).astype(BF16), wfo_ref[fc * j:fc * (j + 1), :],
                       preferred_element_type=F32)
        acc = part if acc is None else acc + part
    o_ref[...] = x1 + _rms(acc, gfpost_ref[...])


def _outffn(x2, ys, w_out, g_post, g_pre, w_fi, w_fo, g_fpost):
    n, d = x2.shape
    d_ff = w_fo.shape[0]
    n_ff_chunks = 2
    assert (d_ff // n_ff_chunks) % LANES == 0
    tile = lambda w: pl.BlockSpec((TM_PROJ, w), lambda i: (i, 0))
    const = lambda a: pl.BlockSpec(a.shape, lambda i: (0, 0), pipeline_mode=pl.Buffered(1))
    params = [w_out, g_post, g_pre, w_fi, w_fo, g_fpost]
    return pl.pallas_call(
        functools.partial(_outffn_kernel, n_ff_chunks),
        out_shape=jax.ShapeDtypeStruct((n, d), F32),
        grid=(n // TM_PROJ,),
        in_specs=[tile(d)] + [tile(G_WIDTH)] * 4 + [const(p) for p in params],
        out_specs=tile(d),
        compiler_params=pltpu.CompilerParams(dimension_semantics=("arbitrary",),
                                             vmem_limit_bytes=VMEM_LIMIT),
        name="out_ffn",
    )(x2, *ys, *params)


def _permute_w_in(w_in):
    g = G_WIDTH
    a0, b0 = 0, 2 * g
    c0 = b0 + 4 * g + 2 * N_DIR * N_HEADS
    d0 = c0 + 4 * g + 2 * N_DIR * N_HEADS
    d_cols = w_in.shape[1] - d0
    pieces = [w_in[:, b0:b0 + 4 * g], w_in[:, c0:c0 + 4 * g], w_in[:, d0:],
              w_in[:, b0 + 4 * g:c0], w_in[:, c0 + 4 * g:d0]]
    used = 8 * g + d_cols + 4 * N_DIR * N_HEADS
    pieces.append(jnp.zeros((w_in.shape[0], OFF_LRU - used), w_in.dtype))
    pieces.append(w_in[:, a0:a0 + 2 * g])
    return jnp.concatenate(pieces, axis=1)


def _layer(x, n_mix_pre, n_mix_post, n_ffn_pre, n_ffn_post, w_in, w_out,
           lru_conv_w, lru_conv_b, lru_gate_w, lru_gate_b, lru_lambda,
           gdn_conv_w, gdn_a_log, gdn_dt_bias, gdn_norm, mlstm_gate_bias, mlstm_norm,
           rwkv_mu, rwkv_w0, rwkv_w_up, rwkv_a0, rwkv_a_up, rwkv_g_up,
           rwkv_k_k, rwkv_k_a, rwkv_r_k, rwkv_gn_w, rwkv_gn_b, ffn_w_in, ffn_w_out):
    bsz, seq, d = x.shape
    x2 = x.reshape(bsz * seq, d)
    u3 = _inproj(x2, _row(n_mix_pre), _permute_w_in(w_in).astype(BF16)).reshape(bsz, seq, P_PAD)
    ys = [
        _lru_mixer(u3, lru_conv_w, lru_conv_b, lru_gate_w, lru_gate_b, lru_lambda),
        _gdn_mixer(u3, gdn_conv_w, gdn_a_log, gdn_dt_bias, gdn_norm),
        _mlstm_mixer(u3, mlstm_gate_bias, mlstm_norm),
        _rwkv_mixer(u3, rwkv_mu, rwkv_w0, rwkv_w_up, rwkv_a0, rwkv_a_up, rwkv_g_up,
                    rwkv_k_k, rwkv_k_a, rwkv_r_k, rwkv_gn_w, rwkv_gn_b),
    ]
    ys = [y.reshape(bsz * seq, G_WIDTH) for y in ys]
    out = _outffn(x2, ys, w_out.astype(BF16), _row(n_mix_post), _row(n_ffn_pre),
                  ffn_w_in.astype(BF16), ffn_w_out.astype(BF16), _row(n_ffn_post))
    return out.reshape(bsz, seq, d)


def kernel(x, norm_mix_pre, norm_mix_post, norm_ffn_pre, norm_ffn_post, w_in, w_out, lru_conv_w, lru_conv_b, lru_gate_w, lru_gate_b, lru_lambda, gdn_conv_w, gdn_a_log, gdn_dt_bias, gdn_norm, mlstm_gate_bias, mlstm_norm, rwkv_mu, rwkv_w0, rwkv_w_up, rwkv_a0, rwkv_a_up, rwkv_g_up, rwkv_k_k, rwkv_k_a, rwkv_r_k, rwkv_gn_w, rwkv_gn_b, ffn_w_in, ffn_w_out):
    bsz, seq, d = x.shape
    assert d == N_HEADS * G_WIDTH and seq % T_BLK == 0 and (bsz * seq) % TM_PROJ == 0
    assert w_in.shape[-1] - (10 * G_WIDTH + 4 * N_DIR * N_HEADS) == 4 * G_WIDTH - RWKV_G_RANK
    stacked = (norm_mix_pre, norm_mix_post, norm_ffn_pre, norm_ffn_post, w_in, w_out,
               lru_conv_w, lru_conv_b, lru_gate_w, lru_gate_b, lru_lambda,
               gdn_conv_w, gdn_a_log, gdn_dt_bias, gdn_norm, mlstm_gate_bias, mlstm_norm,
               rwkv_mu, rwkv_w0, rwkv_w_up, rwkv_a0, rwkv_a_up, rwkv_g_up,
               rwkv_k_k, rwkv_k_a, rwkv_r_k, rwkv_gn_w, rwkv_gn_b, ffn_w_in, ffn_w_out)
    for l in range(w_in.shape[0]):
        x = _layer(x, *(p[l] for p in stacked))
    return x
```

```python
import functools

import jax
import jax.numpy as jnp
from jax import lax
from jax.experimental import pallas as pl
from jax.experimental.pallas import tpu as pltpu

F32 = jnp.float32
BF16 = jnp.bfloat16

HEAD_DIM = 64
N_HEADS = 4
G_WIDTH = N_HEADS * HEAD_DIM
N_DIR = 2
CONV_K = 4
CHUNK = 64
CHUNK_SHIFT = 6
LRU_C = 8.0
GATE_CAP = 15.0
RWKV_RANK_COLS = 128
RWKV_G_RANK = 64
RWKV_GN_EPS = 64e-5
RMS_EPS = 1e-6
L2_EPS = 1e-6

LANES = 128
LANE_SHIFT = 7
HALO = 8
T_BLK = 256
TM_PROJ = 512
VMEM_LIMIT = 56 * 1024 * 1024

OFF_GDN = 0
OFF_MLSTM = 4 * G_WIDTH
OFF_RWKV = 8 * G_WIDTH
OFF_LRU = 12 * G_WIDTH
P_PAD = 14 * G_WIDTH
GATE_TILE_BLK = (OFF_LRU - LANES) // LANES
LANE_ALPHA = RWKV_G_RANK
LANE_BETA = LANE_ALPHA + N_DIR * N_HEADS
LANE_GI = LANE_BETA + N_DIR * N_HEADS
LANE_GF = LANE_GI + N_DIR * N_HEADS


def _bdot(a, b):
    return jnp.dot(a.astype(BF16), b.astype(BF16), preferred_element_type=F32)


def _bdot_nt(a, b):
    return lax.dot_general(a.astype(BF16), b.astype(BF16), (((1,), (1,)), ((), ())),
                           preferred_element_type=F32)


def _bdot_tn(a, b):
    return lax.dot_general(a.astype(BF16), b.astype(BF16), (((0,), (0,)), ((), ())),
                           preferred_element_type=F32)


def _split_bf16(x, n):
    parts, r = [], x
    for _ in range(n):
        p = r.astype(BF16)
        parts.append(p)
        r = r - p.astype(F32)
    return parts


def _dot_sel_lhs(m_bf16, x, n=3):
    acc = None
    for p in _split_bf16(x, n):
        t = jnp.dot(m_bf16, p, preferred_element_type=F32)
        acc = t if acc is None else acc + t
    return acc


def _dot_sel_rhs(x, m_bf16, n=2):
    acc = None
    for p in _split_bf16(x, n):
        t = jnp.dot(p, m_bf16, preferred_element_type=F32)
        acc = t if acc is None else acc + t
    return acc


def _softplus(x):
    return jnp.maximum(x, 0.0) + jnp.log1p(jnp.exp(-jnp.abs(x)))


def _sigmoid(x):
    return 1.0 / (1.0 + jnp.exp(-x))


def _silu(x):
    return x * _sigmoid(x)


def _gelu_tanh(x):
    c = 0.7978845608028654
    return x * (0.5 * (1.0 + jnp.tanh(c * (x + 0.044715 * (x * x * x)))))


def _soft_cap(x):
    return GATE_CAP * jnp.tanh(x / GATE_CAP)


def _rms(x, g):
    return x * lax.rsqrt(jnp.mean(x * x, axis=-1, keepdims=True) + RMS_EPS) * g


def _head_sum_matrix(width):
    i = lax.broadcasted_iota(jnp.int32, (width, width), 0)
    j = lax.broadcasted_iota(jnp.int32, (width, width), 1)
    return jnp.where((i >> CHUNK_SHIFT) == (j >> CHUNK_SHIFT), 1.0, 0.0).astype(BF16)


def _chunk_masks(t, rev):
    i = lax.broadcasted_iota(jnp.int32, (t, t), 0)
    j = lax.broadcasted_iota(jnp.int32, (t, t), 1)
    same = (i >> CHUNK_SHIFT) == (j >> CHUNK_SHIFT)
    if rev:
        return same & (i <= j), same & (i < j)
    return same & (i >= j), same & (i > j)


def _solve_nilpotent(ms, rhss):
    xs = [r + _bdot(m, r) for m, r in zip(ms, rhss)]
    ps = list(ms)
    for _ in range(CHUNK_SHIFT - 1):
        ps = [_bdot(p, p) for p in ps]
        xs = [x + _bdot(p, x) for p, x in zip(ps, xs)]
    return xs


def _chunk_order(t, rev):
    n = t // CHUNK
    return list(range(n - 1, -1, -1)) if rev else list(range(n))


def _fill_halo(xe_ref, hp_ref, hn_ref, body, blk, nt, t):
    xe_ref[0:HALO, :] = jnp.where(blk == 0, 0.0, hp_ref[0])
    xe_ref[HALO:HALO + t, :] = body
    xe_ref[HALO + t:HALO + t + HALO, :] = jnp.where(blk == nt - 1, 0.0, hn_ref[0])


def _inproj_kernel(x_ref, g_ref, w_ref, u_ref):
    h = _rms(x_ref[...], g_ref[...])
    u_ref[...] = jnp.dot(h.astype(BF16), w_ref[...], preferred_element_type=F32)


def _inproj(x2, g, w):
    n, d = x2.shape
    p = w.shape[1]
    return pl.pallas_call(
        _inproj_kernel,
        out_shape=jax.ShapeDtypeStruct((n, p), F32),
        grid=(n // TM_PROJ,),
        in_specs=[pl.BlockSpec((TM_PROJ, d), lambda i: (i, 0)),
                  pl.BlockSpec((1, d), lambda i: (0, 0)),
                  pl.BlockSpec((d, p), lambda i: (0, 0))],
        out_specs=pl.BlockSpec((TM_PROJ, p), lambda i: (i, 0)),
        compiler_params=pltpu.CompilerParams(dimension_semantics=("arbitrary",),
                                             vmem_limit_bytes=VMEM_LIMIT),
        name="in_proj",
    )(x2, g, w)


def _lru_kernel(rev, nt, u_ref, hp_ref, hn_ref, cw_ref, cb_ref, wg_ref, gb_ref, lam_ref, *rest):
    if rev:
        hf_ref, out_ref, xe_ref, a_ref, b_ref, h_ref, carry_ref = rest
    else:
        out_ref, xe_ref, a_ref, b_ref, h_ref, carry_ref = rest
    t = T_BLK
    i = pl.program_id(1)
    blk = nt - 1 - i if rev else i
    x = u_ref[0]
    xb = x[:, :G_WIDTH]
    _fill_halo(xe_ref, hp_ref, hn_ref, xb, blk, nt, t)
    xc = cb_ref[...]
    for j in range(CONV_K):
        xc = xc + cw_ref[j:j + 1, :] * xe_ref[pl.ds(HALO - CONV_K // 2 + j, t), :]
    pre = _bdot(xc, wg_ref[...]) + gb_ref[...]
    r = _sigmoid(pre[:, :G_WIDTH])
    ig = _sigmoid(pre[:, G_WIDTH:])
    log_a = -LRU_C * r * _softplus(-lam_ref[...])
    a_ref[...] = jnp.exp(log_a)
    th = jnp.tanh(log_a)
    b_ref[...] = jnp.sqrt(-2.0 * th / (1.0 - th)) * ig * xc

    @pl.when(i == 0)
    def _():
        carry_ref[...] = jnp.zeros_like(carry_ref)

    def step(s, h):
        row = t - 1 - s if rev else s
        h = a_ref[pl.ds(row, 1), :] * h + b_ref[pl.ds(row, 1), :]
        h_ref[pl.ds(row, 1), :] = h
        return h

    carry_ref[...] = lax.fori_loop(0, t, step, carry_ref[...], unroll=8)
    if rev:
        out_ref[0] = (hf_ref[0] + h_ref[...]) * _gelu_tanh(x[:, G_WIDTH:])
    else:
        out_ref[0] = h_ref[...]


def _gdn_kernel(rev, nt, u_ref, hp_ref, hn_ref, gt_ref, cw_ref, alog_ref, dtb_ref, ng_ref, *rest):
    if rev:
        of_ref, out_ref, xe_ref, s_ref, o_ref = rest
    else:
        out_ref, xe_ref, s_ref, o_ref = rest
    t = T_BLK
    d = 1 if rev else 0
    i = pl.program_id(1)
    blk = nt - 1 - i if rev else i

    @pl.when(i == 0)
    def _():
        s_ref[...] = jnp.zeros_like(s_ref)

    x = u_ref[0]
    _fill_halo(xe_ref, hp_ref, hn_ref, x[:, :3 * G_WIDTH], blk, nt, t)
    acc = None
    for j in range(CONV_K):
        term = cw_ref[j:j + 1, :] * xe_ref[pl.ds(HALO - CONV_K // 2 + j, t), :]
        acc = term if acc is None else acc + term
    qkv = _silu(acc)
    hsum = _head_sum_matrix(G_WIDTH)
    q = qkv[:, :G_WIDTH]
    k = qkv[:, G_WIDTH:2 * G_WIDTH]
    v = qkv[:, 2 * G_WIDTH:]
    q = q * lax.rsqrt(_dot_sel_rhs(q * q, hsum) + L2_EPS) * (HEAD_DIM ** -0.5)
    k = k * lax.rsqrt(_dot_sel_rhs(k * k, hsum) + L2_EPS)

    gt = gt_ref[0]
    g_all = -jnp.exp(alog_ref[...]) * _softplus(gt + dtb_ref[...])
    beta_all = _sigmoid(gt)
    tri, strict = _chunk_masks(t, rev)
    tri_bf = jnp.where(tri, 1.0, 0.0).astype(BF16)
    gc_all = _dot_sel_lhs(tri_bf, g_all)
    gc_rows = gc_all.T

    heads = range(N_HEADS)
    hsl = [slice(HEAD_DIM * h, HEAD_DIM * (h + 1)) for h in heads]
    gc_cs = [gc_all[:, LANE_ALPHA + N_HEADS * d + h:LANE_ALPHA + N_HEADS * d + h + 1] for h in heads]
    ms, rhss, qks, qes = [], [], [], []
    for h in heads:
        la = LANE_ALPHA + N_HEADS * d + h
        lb = LANE_BETA + N_HEADS * d + h
        beta = beta_all[:, lb:lb + 1]
        dec = jnp.exp(jnp.where(tri, gc_cs[h] - gc_rows[la:la + 1, :], -jnp.inf))
        qh, kh, vh = q[:, hsl[h]], k[:, hsl[h]], v[:, hsl[h]]
        kb = kh * beta
        e_gc = jnp.exp(gc_cs[h])
        ms.append(jnp.where(strict, -(_bdot_nt(kb, kh) * dec), 0.0))
        rhss.append(jnp.concatenate([vh * beta, kb * e_gc], axis=1))
        qks.append(_bdot_nt(qh, kh) * dec)
        qes.append(qh * e_gc)
    sols = _solve_nilpotent(ms, rhss)

    order = _chunk_order(t, rev)
    terms = {}
    for c in order:
        sl = slice(CHUNK * c, CHUNK * (c + 1))
        rl = CHUNK * c if rev else CHUNK * (c + 1) - 1
        for h in heads:
            la = LANE_ALPHA + N_HEADS * d + h
            g_last = jnp.broadcast_to(gc_all[rl:rl + 1, :], (CHUNK, LANES))[:, la:la + 1]
            kd = k[sl, hsl[h]] * jnp.exp(g_last - gc_cs[h][sl])
            kuw = _bdot_tn(kd, sols[h][sl])
            quw = _bdot(qks[h][sl, sl], sols[h][sl])
            lhs = jnp.concatenate([qes[h][sl] - quw[:, HEAD_DIM:], -kuw[:, HEAD_DIM:]], axis=0)
            terms[h, c] = (lhs, quw[:, :HEAD_DIM], kuw[:, :HEAD_DIM], jnp.exp(g_last))

    states = [s_ref[h] for h in heads]
    for c in order:
        sl = slice(CHUNK * c, CHUNK * (c + 1))
        for h in heads:
            lhs, o_const, s_const, e_last = terms[h, c]
            both = _bdot(lhs, states[h])
            o_ref[sl, hsl[h]] = both[:CHUNK] + o_const
            states[h] = e_last * states[h] + both[CHUNK:] + s_const
    for h in heads:
        s_ref[h] = states[h]

    if rev:
        o = of_ref[0] + o_ref[...]
        ms = _dot_sel_rhs(o * o, hsum) * (1.0 / HEAD_DIM)
        out_ref[0] = o * lax.rsqrt(ms + RMS_EPS) * ng_ref[...] * _silu(x[:, 3 * G_WIDTH:])
    else:
        out_ref[0] = o_ref[...]


def _mlstm_kernel(rev, nt, u_ref, gt_ref, bias_ref, ng_ref, *rest):
    if rev:
        hf_ref, out_ref, c_ref, m_ref, h_ref = rest
    else:
        out_ref, c_ref, m_ref, h_ref = rest
    t = T_BLK
    d = 1 if rev else 0
    i = pl.program_id(1)

    @pl.when(i == 0)
    def _():
        c_ref[...] = jnp.zeros_like(c_ref)
        m_ref[...] = jnp.zeros_like(m_ref)

    x = u_ref[0]
    q = x[:, :G_WIDTH]
    k = x[:, G_WIDTH:2 * G_WIDTH] * (HEAD_DIM ** -0.5)
    v = x[:, 2 * G_WIDTH:3 * G_WIDTH]

    pre = _soft_cap(gt_ref[0] + bias_ref[...])
    ig_all = pre
    lf_all = -_softplus(-pre)
    tri, _ = _chunk_masks(t, rev)
    tri_bf = jnp.where(tri, 1.0, 0.0).astype(BF16)
    b_all = _dot_sel_lhs(tri_bf, lf_all)
    b_rows = b_all.T
    ig_rows = ig_all.T
    one_col = jnp.where(lax.broadcasted_iota(jnp.int32, (t, HEAD_DIM), 1) == 0, 1.0, 0.0)

    heads = range(N_HEADS)
    order = _chunk_order(t, rev)
    hsl = [slice(HEAD_DIM * h, HEAD_DIM * (h + 1)) for h in heads]

    ig_al = pltpu.roll(ig_all, LANE_GF - LANE_GI, axis=1)
    m_run = m_ref[0:1, :]
    m_in, dec_rows, sc_tiles = {}, {}, {}
    for c in order:
        sl = slice(CHUNK * c, CHUNK * (c + 1))
        rl = CHUNK * c if rev else CHUNK * (c + 1) - 1
        b_last = b_all[rl:rl + 1, :]
        w_end = b_last - b_all[sl] + ig_al[sl]
        m_new = jnp.maximum(b_last + m_run, jnp.max(w_end, axis=0, keepdims=True))
        m_in[c] = m_run
        dec_rows[c] = jnp.exp(b_last + m_run - m_new)
        sc_tiles[c] = jnp.exp(w_end - m_new)
        m_run = m_new
    m_ref[...] = jnp.broadcast_to(m_run, m_ref.shape)
    inter_all = jnp.concatenate(
        [b_all[CHUNK * c:CHUNK * (c + 1)] + m_in[c] for c in range(t // CHUNK)], axis=0)

    lfs = [LANE_GF + N_HEADS * d + h for h in heads]
    khs = [k[:, hsl[h]] for h in heads]
    v_exts = [jnp.concatenate([v[:, hsl[h]], one_col], axis=1) for h in heads]
    d_logs = []
    for h in heads:
        li = LANE_GI + N_HEADS * d + h
        lf = lfs[h]
        d_logs.append(jnp.where(tri, b_all[:, lf:lf + 1] - b_rows[lf:lf + 1, :] + ig_rows[li:li + 1, :],
                                -jnp.inf))
    d_maxs = [jnp.max(dl, axis=1, keepdims=True) for dl in d_logs]
    inters = [inter_all[:, lf:lf + 1] for lf in lfs]
    m_ts = [jnp.maximum(inters[h], d_maxs[h]) for h in heads]
    ps = [_bdot_nt(q[:, hsl[h]], khs[h]) * jnp.exp(d_logs[h] - m_ts[h]) for h in heads]
    intras = [_bdot(ps[h], v_exts[h]) for h in heads]
    s_inters = [jnp.exp(inters[h] - m_ts[h]) for h in heads]
    floors = [jnp.exp(-m_ts[h]) for h in heads]
    n_chunks = t // CHUNK
    dec_stack = jnp.concatenate([dec_rows[c] for c in range(n_chunks)]
                                + [jnp.zeros((HALO - n_chunks, LANES), F32)], axis=0)
    sel_r = lax.broadcasted_iota(jnp.int32, (LANES, N_HEADS * LANES), 0)
    sel_c = lax.broadcasted_iota(jnp.int32, (LANES, N_HEADS * LANES), 1)
    sel = jnp.where(sel_r == LANE_GF + N_HEADS * d + (sel_c >> LANE_SHIFT), 1.0, 0.0).astype(BF16)
    dec_rep = _dot_sel_rhs(dec_stack, sel, n=3)
    incs, decs = {}, {}
    for c in order:
        sl = slice(CHUNK * c, CHUNK * (c + 1))
        for h in heads:
            lf = lfs[h]
            decs[h, c] = dec_rep[c:c + 1, LANES * h:LANES * (h + 1)]
            incs[h, c] = _bdot_tn(sc_tiles[c][:, lf:lf + 1] * khs[h][sl], v_exts[h][sl])

    cns = [c_ref[h] for h in heads]
    for c in order:
        sl = slice(CHUNK * c, CHUNK * (c + 1))
        for h in heads:
            nd = intras[h][sl] + s_inters[h][sl] * _bdot(q[sl, hsl[h]], cns[h])
            den = jnp.maximum(jnp.abs(nd[:, HEAD_DIM:HEAD_DIM + 1]), floors[h][sl])
            h_ref[sl, hsl[h]] = nd[:, :HEAD_DIM] / den
            cns[h] = decs[h, c] * cns[h] + incs[h, c]
    for h in heads:
        c_ref[h] = cns[h]

    if rev:
        hh = hf_ref[0] + h_ref[...]
        ms = _dot_sel_rhs(hh * hh, _head_sum_matrix(G_WIDTH)) * (1.0 / HEAD_DIM)
        out_ref[0] = hh * lax.rsqrt(ms + RMS_EPS) * ng_ref[...] * _sigmoid(x[:, 3 * G_WIDTH:])
    else:
        out_ref[0] = h_ref[...]


def _rwkv_kernel(rev, nt, u_ref, hp_ref, hn_ref, mu_ref, wup_ref, w0_ref, a0_ref, gup_ref,
                 kk_ref, ka_ref, rk_ref, gnw_ref, gnb_ref, *rest):
    if rev:
        yf_ref, out_ref, xe_ref, s_ref, y_ref = rest
    else:
        out_ref, xe_ref, s_ref, y_ref = rest
    t = T_BLK
    i = pl.program_id(1)
    blk = nt - 1 - i if rev else i

    @pl.when(i == 0)
    def _():
        s_ref[...] = jnp.zeros_like(s_ref)

    x = u_ref[0]
    _fill_halo(xe_ref, hp_ref, hn_ref, x, blk, nt, t)
    x_prev = xe_ref[pl.ds(HALO - 1, t), :]
    x_next = xe_ref[pl.ds(HALO + 1, t), :]
    us = x + mu_ref[0:1, :] * (x_prev - x) + mu_ref[1:2, :] * (x_next - x)
    r = us[:, :G_WIDTH]
    k = us[:, G_WIDTH:2 * G_WIDTH]
    v = us[:, 2 * G_WIDTH:3 * G_WIDTH]
    low = us[:, 3 * G_WIDTH:3 * G_WIDTH + RWKV_RANK_COLS]
    lane = lax.broadcasted_iota(jnp.int32, (t, RWKV_RANK_COLS), 1)
    low = jnp.where(lane < RWKV_RANK_COLS // 2, jnp.tanh(low), low)
    za = _bdot(low, wup_ref[...])
    z_w = w0_ref[...] + za[:, :G_WIDTH]
    lw = -jnp.exp(-_softplus(-z_w) - 0.5)
    a = _sigmoid(a0_ref[...] + za[:, G_WIDTH:])
    gate = _bdot(_sigmoid(us[:, 3 * G_WIDTH + RWKV_RANK_COLS:]), gup_ref[...])
    hsum = _head_sum_matrix(G_WIDTH)
    kk = k * kk_ref[...]
    kk = kk * lax.rsqrt(_dot_sel_rhs(kk * kk, hsum) + L2_EPS)
    k_dir = k * (1.0 + (a - 1.0) * ka_ref[...])
    akk = a * kk

    tri, strict = _chunk_masks(t, rev)
    tri_bf = jnp.where(tri, 1.0, 0.0).astype(BF16)
    gc = _dot_sel_lhs(tri_bf, lw)
    e_inc = jnp.exp(gc)
    e_exc = jnp.exp(gc - lw)
    e_neg = jnp.exp(-gc)
    left_a = -kk * e_exc
    left_r = r * e_inc
    right_b = akk * e_neg
    right_k = k_dir * e_neg

    heads = range(N_HEADS)
    hsl = [slice(HEAD_DIM * h, HEAD_DIM * (h + 1)) for h in heads]
    a_abs, a_rbs, rhss, yvs = [], [], [], []
    for h in heads:
        hs = hsl[h]
        gram = _bdot_nt(jnp.concatenate([left_a[:, hs], left_r[:, hs]], axis=0),
                        jnp.concatenate([right_b[:, hs], right_k[:, hs]], axis=0))
        a_abs.append(jnp.where(strict, gram[:t, :t], 0.0))
        a_rbs.append(jnp.where(tri, gram[t:, :t], 0.0))
        a_k = jnp.concatenate([jnp.where(strict, gram[:t, t:], 0.0),
                               jnp.where(tri, gram[t:, t:], 0.0)], axis=0)
        akv = _bdot(a_k, v[:, hs])
        rhss.append(jnp.concatenate([left_a[:, hs], akv[:t]], axis=1))
        yvs.append(akv[t:])
    sols = _solve_nilpotent(a_abs, rhss)

    order = _chunk_order(t, rev)
    terms = {}
    for c in order:
        sl = slice(CHUNK * c, CHUNK * (c + 1))
        rl = CHUNK * c if rev else CHUNK * (c + 1) - 1
        for h in heads:
            hs = hsl[h]
            sol_c = sols[h][sl]
            g_last = gc[rl:rl + 1, hs]
            tail = jnp.exp(g_last - gc[sl, hs])
            bd = akk[sl, hs] * tail
            kd = k_dir[sl, hs] * tail
            ar = _bdot(a_rbs[h][sl, sl], sol_c)
            y_lhs = left_r[sl, hs] + ar[:, :HEAD_DIM]
            y_const = ar[:, HEAD_DIM:] + yvs[h][sl]
            phi = _bdot_tn(sol_c[:, :HEAD_DIM], bd)
            s_const = _bdot_tn(jnp.concatenate([sol_c[:, HEAD_DIM:], v[sl, hs]], axis=0),
                               jnp.concatenate([bd, kd], axis=0))
            terms[h, c] = (y_lhs, y_const, phi, s_const, jnp.exp(g_last))

    states = [s_ref[h] for h in heads]
    for c in order:
        sl = slice(CHUNK * c, CHUNK * (c + 1))
        for h in heads:
            y_lhs, y_const, phi, s_const, e_last = terms[h, c]
            y_ref[sl, hsl[h]] = _bdot_nt(y_lhs, states[h]) + y_const
            states[h] = states[h] * e_last + _bdot(states[h], phi) + s_const
    for h in heads:
        s_ref[h] = states[h]

    if rev:
        y = yf_ref[0] + y_ref[...]
        mean = _dot_sel_rhs(y, hsum) * (1.0 / HEAD_DIM)
        dlt = y - mean
        var = _dot_sel_rhs(dlt * dlt, hsum) * (1.0 / HEAD_DIM)
        yn = dlt * lax.rsqrt(var + RWKV_GN_EPS) * gnw_ref[...] + gnb_ref[...]
        bonus = _dot_sel_rhs(r * k * rk_ref[...], hsum) * v
        out_ref[0] = (yn + bonus) * gate
    else:
        out_ref[0] = y_ref[...]


def _mixer_call(body, rev, u3, col_blk, col_w, halo_w, use_gates, params, prev_out, scratch, name):
    bsz, seq, _ = u3.shape
    t = T_BLK
    nt = seq // t
    rows = t // HALO

    def blk_of(i):
        return nt - 1 - i if rev else i

    in_specs = [pl.BlockSpec((1, t, col_w), lambda b, i: (b, blk_of(i), col_blk))]
    args = [u3]
    if halo_w:
        halo_blk = col_blk * col_w // halo_w
        in_specs += [
            pl.BlockSpec((1, HALO, halo_w),
                         lambda b, i: (b, jnp.maximum(blk_of(i) * rows - 1, 0), halo_blk)),
            pl.BlockSpec((1, HALO, halo_w),
                         lambda b, i: (b, jnp.minimum((blk_of(i) + 1) * rows, seq // HALO - 1), halo_blk)),
        ]
        args += [u3, u3]
    if use_gates:
        in_specs.append(pl.BlockSpec((1, t, LANES), lambda b, i: (b, blk_of(i), GATE_TILE_BLK)))
        args.append(u3)
    for p in params:
        in_specs.append(pl.BlockSpec(p.shape, lambda b, i, nd=p.ndim: (0,) * nd))
        args.append(p)
    if rev:
        in_specs.append(pl.BlockSpec((1, t, G_WIDTH), lambda b, i: (b, blk_of(i), 0)))
        args.append(prev_out)
    return pl.pallas_call(
        functools.partial(body, rev, nt),
        out_shape=jax.ShapeDtypeStruct((bsz, seq, G_WIDTH), F32),
        grid=(bsz, nt),
        in_specs=in_specs,
        out_specs=pl.BlockSpec((1, t, G_WIDTH), lambda b, i: (b, blk_of(i), 0)),
        scratch_shapes=scratch,
        compiler_params=pltpu.CompilerParams(dimension_semantics=("arbitrary", "arbitrary"),
                                             vmem_limit_bytes=VMEM_LIMIT),
        name=name + ("_bwd" if rev else "_fwd"),
    )(*args)


def _both_dirs(body, u3, col_blk, col_w, halo_w, use_gates, params_of_dir, scratch, name):
    fwd = _mixer_call(body, False, u3, col_blk, col_w, halo_w, use_gates, params_of_dir(0), None,
                      scratch, name)
    return _mixer_call(body, True, u3, col_blk, col_w, halo_w, use_gates, params_of_dir(1), fwd,
                       scratch, name)


def _row(v):
    return v.reshape(1, -1).astype(F32)


def _gate_row(values, lane0):
    flat = values.reshape(-1).astype(F32)
    return jnp.zeros((1, LANES), F32).at[0, lane0:lane0 + flat.shape[0]].set(flat)


def _lru_mixer(u3, conv_w, conv_b, gate_w, gate_b, lam):
    t = T_BLK
    eye = jnp.eye(N_HEADS, dtype=F32)

    def params(d):
        dense = jnp.einsum('gnio,nm->nigmo', gate_w[d], eye).reshape(G_WIDTH, 2 * G_WIDTH)
        return [conv_w, _row(conv_b), dense.astype(BF16), _row(gate_b[d]), _row(lam[d])]

    scratch = [pltpu.VMEM((t + 2 * HALO, G_WIDTH), F32), pltpu.VMEM((t, G_WIDTH), F32),
               pltpu.VMEM((t, G_WIDTH), F32), pltpu.VMEM((t, G_WIDTH), F32),
               pltpu.VMEM((1, G_WIDTH), F32)]
    return _both_dirs(_lru_kernel, u3, OFF_LRU // (2 * G_WIDTH), 2 * G_WIDTH, G_WIDTH, False,
                      params, scratch, "rglru")


def _gdn_mixer(u3, conv_w, a_log, dt_bias, norm_g):
    t = T_BLK
    shared = [conv_w, _gate_row(a_log, LANE_ALPHA), _gate_row(dt_bias, LANE_ALPHA),
              _row(jnp.tile(norm_g, N_HEADS))]
    scratch = [pltpu.VMEM((t + 2 * HALO, 3 * G_WIDTH), F32),
               pltpu.VMEM((N_HEADS, HEAD_DIM, HEAD_DIM), F32), pltpu.VMEM((t, G_WIDTH), F32)]
    return _both_dirs(_gdn_kernel, u3, OFF_GDN // (4 * G_WIDTH), 4 * G_WIDTH, 3 * G_WIDTH, True,
                      lambda d: shared, scratch, "gdn")


def _mlstm_mixer(u3, gate_bias, norm_g):
    t = T_BLK
    bias = (_gate_row(gate_bias[:, 0], LANE_GI) + _gate_row(gate_bias[:, 1], LANE_GF))
    shared = [bias, _row(norm_g)]
    scratch = [pltpu.VMEM((N_HEADS, HEAD_DIM, 2 * HEAD_DIM), F32),
               pltpu.VMEM((HALO, LANES), F32), pltpu.VMEM((t, G_WIDTH), F32)]
    return _both_dirs(_mlstm_kernel, u3, OFF_MLSTM // (4 * G_WIDTH), 4 * G_WIDTH, 0, True,
                      lambda d: shared, scratch, "mlstm")


def _rwkv_mixer(u3, mu, w0, w_up, a0, a_up, g_up, k_k, k_a, r_k, gn_w, gn_b):
    t = T_BLK
    rank = w_up.shape[1]
    mu_pad = jnp.pad(mu, ((0, 0), (0, 4 * G_WIDTH - mu.shape[1])))
    gup_pad = jnp.pad(g_up, ((0, LANES - g_up.shape[0]), (0, 0))).astype(BF16)

    def params(d):
        wup = jnp.zeros((RWKV_RANK_COLS, 2 * G_WIDTH), F32)
        wup = wup.at[rank * d:rank * (d + 1), :G_WIDTH].set(w_up[d])
        half = RWKV_RANK_COLS // 2
        wup = wup.at[half + rank * d:half + rank * (d + 1), G_WIDTH:].set(a_up[d])
        return [mu_pad, wup.astype(BF16), _row(w0[d]), _row(a0[d]), gup_pad, _row(k_k), _row(k_a),
                _row(r_k), _row(gn_w), _row(gn_b)]

    scratch = [pltpu.VMEM((t + 2 * HALO, 4 * G_WIDTH), F32),
               pltpu.VMEM((N_HEADS, HEAD_DIM, HEAD_DIM), F32), pltpu.VMEM((t, G_WIDTH), F32)]
    return _both_dirs(_rwkv_kernel, u3, OFF_RWKV // (4 * G_WIDTH), 4 * G_WIDTH, 4 * G_WIDTH, False,
                      params, scratch, "rwkv")


def _outffn_kernel(n_ff_chunks, x_ref, ya_ref, yb_ref, yc_ref, yd_ref, wout_ref, gpost_ref, gpre_ref,
                   wfi_ref, wfo_ref, gfpost_ref, o_ref):
    y = jnp.concatenate([ya_ref[...], yb_ref[...], yc_ref[...], yd_ref[...]], axis=-1)
    mixed = jnp.dot(y.astype(BF16), wout_ref[...], preferred_element_type=F32)
    x1 = x_ref[...] + _rms(mixed, gpost_ref[...])
    h = _rms(x1, gpre_ref[...]).astype(BF16)
    d_ff = wfo_ref.shape[0]
    fc = d_ff // n_ff_chunks
    acc = None
    for j in range(n_ff_chunks):
        gate = jnp.dot(h, wfi_ref[:, fc * j:fc * (j + 1)], preferred_element_type=F32)
        up = jnp.dot(h, wfi_ref[:, d_ff + fc * j:d_ff + fc * (j + 1)], preferred_element_type=F32)
        part = jnp.dot((_silu(gate) * up).astype(BF16), wfo_ref[fc * j:fc * (j + 1), :],
                       preferred_element_type=F32)
        acc = part if acc is None else acc + part
    o_ref[...] = x1 + _rms(acc, gfpost_ref[...])


def _outffn(x2, ys, w_out, g_post, g_pre, w_fi, w_fo, g_fpost):
    n, d = x2.shape
    d_ff = w_fo.shape[0]
    n_ff_chunks = 2
    assert (d_ff // n_ff_chunks) % LANES == 0
    tile = lambda w: pl.BlockSpec((TM_PROJ, w), lambda i: (i, 0))
    const = lambda a: pl.BlockSpec(a.shape, lambda i: (0, 0), pipeline_mode=pl.Buffered(1))
    params = [w_out, g_post, g_pre, w_fi, w_fo, g_fpost]
    return pl.pallas_call(
        functools.partial(_outffn_kernel, n_ff_chunks),
        out_shape=jax.ShapeDtypeStruct((n, d), F32),
        grid=(n // TM_PROJ,),
        in_specs=[tile(d)] + [tile(G_WIDTH)] * 4 + [const(p) for p in params],
        out_specs=tile(d),
        compiler_params=pltpu.CompilerParams(dimension_semantics=("arbitrary",),
                                             vmem_limit_bytes=VMEM_LIMIT),
        name="out_ffn",
    )(x2, *ys, *params)


def _permute_w_in(w_in):
    g = G_WIDTH
    a0, b0 = 0, 2 * g
    c0 = b0 + 4 * g + 2 * N_DIR * N_HEADS
    d0 = c0 + 4 * g + 2 * N_DIR * N_HEADS
    d_cols = w_in.shape[1] - d0
    pieces = [w_in[:, b0:b0 + 4 * g], w_in[:, c0:c0 + 4 * g], w_in[:, d0:],
              w_in[:, b0 + 4 * g:c0], w_in[:, c0 + 4 * g:d0]]
    used = 8 * g + d_cols + 4 * N_DIR * N_HEADS
    pieces.append(jnp.zeros((w_in.shape[0], OFF_LRU - used), w_in.dtype))
    pieces.append(w_in[:, a0:a0 + 2 * g])
    return jnp.concatenate(pieces, axis=1)


def _layer(x, n_mix_pre, n_mix_post, n_ffn_pre, n_ffn_post, w_in, w_out,
           lru_conv_w, lru_conv_b, lru_gate_w, lru_gate_b, lru_lambda,
           gdn_conv_w, gdn_a_log, gdn_dt_bias, gdn_norm, mlstm_gate_bias, mlstm_norm,
           rwkv_mu, rwkv_w0, rwkv_w_up, rwkv_a0, rwkv_a_up, rwkv_g_up,
           rwkv_k_k, rwkv_k_a, rwkv_r_k, rwkv_gn_w, rwkv_gn_b, ffn_w_in, ffn_w_out):
    bsz, seq, d = x.shape
    x2 = x.reshape(bsz * seq, d)
    u3 = _inproj(x2, _row(n_mix_pre), _permute_w_in(w_in).astype(BF16)).reshape(bsz, seq, P_PAD)
    ys = [
        _lru_mixer(u3, lru_conv_w, lru_conv_b, lru_gate_w, lru_gate_b, lru_lambda),
        _gdn_mixer(u3, gdn_conv_w, gdn_a_log, gdn_dt_bias, gdn_norm),
        _mlstm_mixer(u3, mlstm_gate_bias, mlstm_norm),
        _rwkv_mixer(u3, rwkv_mu, rwkv_w0, rwkv_w_up, rwkv_a0, rwkv_a_up, rwkv_g_up,
                    rwkv_k_k, rwkv_k_a, rwkv_r_k, rwkv_gn_w, rwkv_gn_b),
    ]
    ys = [y.reshape(bsz * seq, G_WIDTH) for y in ys]
    out = _outffn(x2, ys, w_out.astype(BF16), _row(n_mix_post), _row(n_ffn_pre),
                  ffn_w_in.astype(BF16), ffn_w_out.astype(BF16), _row(n_ffn_post))
    return out.reshape(bsz, seq, d)


def kernel(x, norm_mix_pre, norm_mix_post, norm_ffn_pre, norm_ffn_post, w_in, w_out, lru_conv_w, lru_conv_b, lru_gate_w, lru_gate_b, lru_lambda, gdn_conv_w, gdn_a_log, gdn_dt_bias, gdn_norm, mlstm_gate_bias, mlstm_norm, rwkv_mu, rwkv_w0, rwkv_w_up, rwkv_a0, rwkv_a_up, rwkv_g_up, rwkv_k_k, rwkv_k_a, rwkv_r_k, rwkv_gn_w, rwkv_gn_b, ffn_w_in, ffn_w_out):
    bsz, seq, d = x.shape
    assert d == N_HEADS * G_WIDTH and seq % T_BLK == 0 and (bsz * seq) % TM_PROJ == 0
    assert w_in.shape[-1] - (10 * G_WIDTH + 4 * N_DIR * N_HEADS) == 4 * G_WIDTH - RWKV_G_RANK
    stacked = (norm_mix_pre, norm_mix_post, norm_ffn_pre, norm_ffn_post, w_in, w_out,
               lru_conv_w, lru_conv_b, lru_gate_w, lru_gate_b, lru_lambda,
               gdn_conv_w, gdn_a_log, gdn_dt_bias, gdn_norm, mlstm_gate_bias, mlstm_norm,
               rwkv_mu, rwkv_w0, rwkv_w_up, rwkv_a0, rwkv_a_up, rwkv_g_up,
               rwkv_k_k, rwkv_k_a, rwkv_r_k, rwkv_gn_w, rwkv_gn_b, ffn_w_in, ffn_w_out)
    for l in range(w_in.shape[0]):
        x = _layer(x, *(p[l] for p in stacked))
    return x
```

```python
import functools

import jax
import jax.numpy as jnp
from jax import lax
from jax.experimental import pallas as pl
from jax.experimental.pallas import tpu as pltpu

F32 = jnp.float32
BF16 = jnp.bfloat16

HEAD_DIM = 64
N_HEADS = 4
G_WIDTH = N_HEADS * HEAD_DIM
N_DIR = 2
CONV_K = 4
CHUNK = 64
CHUNK_SHIFT = 6
LRU_C = 8.0
GATE_CAP = 15.0
RWKV_RANK_COLS = 128
RWKV_G_RANK = 64
RWKV_GN_EPS = 64e-5
RMS_EPS = 1e-6
L2_EPS = 1e-6

LANES = 128
LANE_SHIFT = 7
HALO = 8
T_BLK = 256
T_LRU = 1024
ROWS_CHUNKED = 2
ROW_LAG = 1
TM_PROJ = 512
VMEM_LIMIT = 56 * 1024 * 1024

OFF_GDN = 0
OFF_MLSTM = 4 * G_WIDTH
OFF_RWKV = 8 * G_WIDTH
OFF_LRU = 12 * G_WIDTH
P_PAD = 14 * G_WIDTH
GATE_TILE_BLK = (OFF_LRU - LANES) // LANES
LANE_ALPHA = RWKV_G_RANK
LANE_BETA = LANE_ALPHA + N_DIR * N_HEADS
LANE_GI = LANE_BETA + N_DIR * N_HEADS
LANE_GF = LANE_GI + N_DIR * N_HEADS


def _bdot(a, b):
    return jnp.dot(a.astype(BF16), b.astype(BF16), preferred_element_type=F32)


def _bdot_nt(a, b):
    return lax.dot_general(a.astype(BF16), b.astype(BF16), (((1,), (1,)), ((), ())),
                           preferred_element_type=F32)


def _bdot_tn(a, b):
    return lax.dot_general(a.astype(BF16), b.astype(BF16), (((0,), (0,)), ((), ())),
                           preferred_element_type=F32)


def _split_bf16(x, n):
    parts, r = [], x
    for _ in range(n):
        p = r.astype(BF16)
        parts.append(p)
        r = r - p.astype(F32)
    return parts


def _dot_sel_lhs(m_bf16, x, n=3):
    acc = None
    for p in _split_bf16(x, n):
        t = jnp.dot(m_bf16, p, preferred_element_type=F32)
        acc = t if acc is None else acc + t
    return acc


def _dot_sel_rhs(x, m_bf16, n=2):
    acc = None
    for p in _split_bf16(x, n):
        t = jnp.dot(p, m_bf16, preferred_element_type=F32)
        acc = t if acc is None else acc + t
    return acc


def _softplus(x):
    return jnp.maximum(x, 0.0) + jnp.log1p(jnp.exp(-jnp.abs(x)))


def _sigmoid(x):
    return 1.0 / (1.0 + jnp.exp(-x))


def _silu(x):
    return x * _sigmoid(x)


def _gelu_tanh(x):
    c = 0.7978845608028654
    return x * (0.5 * (1.0 + jnp.tanh(c * (x + 0.044715 * (x * x * x)))))


def _soft_cap(x):
    return GATE_CAP * jnp.tanh(x / GATE_CAP)


def _rms(x, g):
    return x * lax.rsqrt(jnp.mean(x * x, axis=-1, keepdims=True) + RMS_EPS) * g


def _head_sum_matrix(width):
    i = lax.broadcasted_iota(jnp.int32, (width, width), 0)
    j = lax.broadcasted_iota(jnp.int32, (width, width), 1)
    return jnp.where((i >> CHUNK_SHIFT) == (j >> CHUNK_SHIFT), 1.0, 0.0).astype(BF16)


def _chunk_masks(t, rev):
    i = lax.broadcasted_iota(jnp.int32, (t, t), 0)
    j = lax.broadcasted_iota(jnp.int32, (t, t), 1)
    same = (i >> CHUNK_SHIFT) == (j >> CHUNK_SHIFT)
    if rev:
        return same & (i <= j), same & (i < j)
    return same & (i >= j), same & (i > j)


def _solve_nilpotent(ms, rhss):
    xs = [r + _bdot(m, r) for m, r in zip(ms, rhss)]
    ps = list(ms)
    for _ in range(CHUNK_SHIFT - 1):
        yield
        ps = [_bdot(p, p) for p in ps]
        xs = [x + _bdot(p, x) for p, x in zip(ps, xs)]
    return xs


def _interleave(programs, lag):
    done = [False] * len(programs)
    step = 0
    while not all(done):
        for k, prog in enumerate(programs):
            if done[k] or step < lag * k:
                continue
            try:
                next(prog)
            except StopIteration:
                done[k] = True
        step += 1


def _chunk_order(t, rev):
    n = t // CHUNK
    return list(range(n - 1, -1, -1)) if rev else list(range(n))


def _fill_halo(xe_ref, prev_rows, next_rows, body, blk, nt, t):
    xe_ref[0:HALO, :] = jnp.where(blk == 0, 0.0, prev_rows)
    xe_ref[HALO:HALO + t, :] = body
    xe_ref[HALO + t:HALO + t + HALO, :] = jnp.where(blk == nt - 1, 0.0, next_rows)


def _inproj_kernel(x_ref, g_ref, w_ref, alog_ref, dtb_ref, bias_ref, u_ref):
    h = _rms(x_ref[...], g_ref[...])
    u_ref[...] = jnp.dot(h.astype(BF16), w_ref[...], preferred_element_type=F32)
    c0 = GATE_TILE_BLK * LANES
    raw = u_ref[:, c0:c0 + LANES]
    lane = lax.broadcasted_iota(jnp.int32, raw.shape, 1)
    capped = _soft_cap(raw + bias_ref[...])
    done = jnp.where(lane < LANE_BETA, -jnp.exp(alog_ref[...]) * _softplus(raw + dtb_ref[...]),
                     jnp.where(lane < LANE_GI, _sigmoid(raw),
                               jnp.where(lane < LANE_GF, capped, -_softplus(-capped))))
    u_ref[:, c0:c0 + LANES] = jnp.where((lane >= LANE_ALPHA) & (lane < LANE_GF + N_DIR * N_HEADS), done, raw)


def _inproj(x2, g, w, alog_row, dtb_row, bias_row):
    n, d = x2.shape
    p = w.shape[1]
    const = lambda a: pl.BlockSpec(a.shape, lambda i: (0, 0))
    return pl.pallas_call(
        _inproj_kernel,
        out_shape=jax.ShapeDtypeStruct((n, p), F32),
        grid=(n // TM_PROJ,),
        in_specs=[pl.BlockSpec((TM_PROJ, d), lambda i: (i, 0)), const(g), const(w),
                  const(alog_row), const(dtb_row), const(bias_row)],
        out_specs=pl.BlockSpec((TM_PROJ, p), lambda i: (i, 0)),
        compiler_params=pltpu.CompilerParams(dimension_semantics=("arbitrary",),
                                             vmem_limit_bytes=VMEM_LIMIT),
        name="in_proj",
    )(x2, g, w, alog_row, dtb_row, bias_row)


def _lru_kernel(rev, nt, u_ref, hp_ref, hn_ref, cw_ref, cb_ref, wg_ref, gb_ref, lam_ref, *rest):
    if rev:
        hf_ref, out_ref, xe_ref, a_ref, b_ref, h_ref, carry_ref = rest
    else:
        out_ref, xe_ref, a_ref, b_ref, h_ref, carry_ref = rest
    t = u_ref.shape[1]
    i = pl.program_id(1)
    blk = nt - 1 - i if rev else i
    x = u_ref[0]
    xb = x[:, :G_WIDTH]
    _fill_halo(xe_ref.at[0], hp_ref[0], hn_ref[0], xb, blk, nt, t)
    xc = cb_ref[...]
    for j in range(CONV_K):
        xc = xc + cw_ref[j:j + 1, :] * xe_ref[0, pl.ds(HALO - CONV_K // 2 + j, t), :]
    pre = _bdot(xc, wg_ref[...]) + gb_ref[...]
    r = _sigmoid(pre[:, :G_WIDTH])
    ig = _sigmoid(pre[:, G_WIDTH:])
    log_a = -LRU_C * r * _softplus(-lam_ref[...])
    a_ref[...] = jnp.exp(log_a)
    th = jnp.tanh(log_a)
    b_ref[...] = jnp.sqrt(-2.0 * th / (1.0 - th)) * ig * xc

    @pl.when(i == 0)
    def _():
        carry_ref[...] = jnp.zeros_like(carry_ref)

    def step(s, h):
        row = t - 1 - s if rev else s
        h = a_ref[pl.ds(row, 1), :] * h + b_ref[pl.ds(row, 1), :]
        h_ref[pl.ds(row, 1), :] = h
        return h

    carry_ref[...] = lax.fori_loop(0, t, step, carry_ref[...], unroll=8)
    if rev:
        out_ref[0] = (hf_ref[0] + h_ref[...]) * _gelu_tanh(x[:, G_WIDTH:])
    else:
        out_ref[0] = h_ref[...]


def _gdn_kernel(rev, nt, u_ref, hp_ref, hn_ref, gt_ref, cw_ref, ng_ref, *rest):
    if rev:
        of_ref, out_ref, xe_ref, s_ref, o_ref = rest
    else:
        out_ref, xe_ref, s_ref, o_ref = rest
    t = T_BLK
    d = 1 if rev else 0
    i = pl.program_id(1)
    blk = nt - 1 - i if rev else i
    heads = range(N_HEADS)
    hsl = [slice(HEAD_DIM * h, HEAD_DIM * (h + 1)) for h in heads]
    order = _chunk_order(t, rev)

    @pl.when(i == 0)
    def _():
        s_ref[...] = jnp.zeros_like(s_ref)

    def row(r):
        x = u_ref[r]
        _fill_halo(xe_ref.at[r], hp_ref[r], hn_ref[r], x[:, :3 * G_WIDTH], blk, nt, t)
        acc = None
        for j in range(CONV_K):
            term = cw_ref[j:j + 1, :] * xe_ref[r, pl.ds(HALO - CONV_K // 2 + j, t), :]
            acc = term if acc is None else acc + term
        qkv = _silu(acc)
        yield
        hsum = _head_sum_matrix(G_WIDTH)
        q = qkv[:, :G_WIDTH]
        k = qkv[:, G_WIDTH:2 * G_WIDTH]
        v = qkv[:, 2 * G_WIDTH:]
        q = q * lax.rsqrt(_dot_sel_rhs(q * q, hsum) + L2_EPS) * (HEAD_DIM ** -0.5)
        k = k * lax.rsqrt(_dot_sel_rhs(k * k, hsum) + L2_EPS)
        g_all = gt_ref[r]
        beta_all = g_all
        tri, strict = _chunk_masks(t, rev)
        tri_bf = jnp.where(tri, 1.0, 0.0).astype(BF16)
        gc_all = _dot_sel_lhs(tri_bf, g_all, n=2)
        gc_rows = gc_all.T
        yield

        gc_cs = [gc_all[:, LANE_ALPHA + N_HEADS * d + h:LANE_ALPHA + N_HEADS * d + h + 1] for h in heads]
        ms, rhss, qks, qes = [], [], [], []
        for h in heads:
            la = LANE_ALPHA + N_HEADS * d + h
            lb = LANE_BETA + N_HEADS * d + h
            beta = beta_all[:, lb:lb + 1]
            dec = jnp.exp(jnp.where(tri, gc_cs[h] - gc_rows[la:la + 1, :], -jnp.inf))
            qh, kh, vh = q[:, hsl[h]], k[:, hsl[h]], v[:, hsl[h]]
            kb = kh * beta
            e_gc = jnp.exp(gc_cs[h])
            ms.append(jnp.where(strict, -(_bdot_nt(kb, kh) * dec), 0.0))
            rhss.append(jnp.concatenate([vh * beta, kb * e_gc], axis=1))
            qks.append(_bdot_nt(qh, kh) * dec)
            qes.append(qh * e_gc)
        yield
        sols = yield from _solve_nilpotent(ms, rhss)
        yield

        terms = {}
        for c in order:
            sl = slice(CHUNK * c, CHUNK * (c + 1))
            rl = CHUNK * c if rev else CHUNK * (c + 1) - 1
            for h in heads:
                la = LANE_ALPHA + N_HEADS * d + h
                g_last = jnp.broadcast_to(gc_all[rl:rl + 1, :], (CHUNK, LANES))[:, la:la + 1]
                kd = k[sl, hsl[h]] * jnp.exp(g_last - gc_cs[h][sl])
                kuw = _bdot_tn(kd, sols[h][sl])
                quw = _bdot(qks[h][sl, sl], sols[h][sl])
                lhs = jnp.concatenate([qes[h][sl] - quw[:, HEAD_DIM:], -kuw[:, HEAD_DIM:]], axis=0)
                terms[h, c] = (lhs, quw[:, :HEAD_DIM], kuw[:, :HEAD_DIM], jnp.exp(g_last))
        yield

        states = [s_ref[N_HEADS * r + h] for h in heads]
        for c in order:
            sl = slice(CHUNK * c, CHUNK * (c + 1))
            for h in heads:
                lhs, o_const, s_const, e_last = terms[h, c]
                both = _bdot(lhs, states[h])
                o_ref[r, sl, hsl[h]] = both[:CHUNK] + o_const
                states[h] = e_last * states[h] + both[CHUNK:] + s_const
            yield
        for h in heads:
            s_ref[N_HEADS * r + h] = states[h]

        if rev:
            o = of_ref[r] + o_ref[r]
            ms_o = _dot_sel_rhs(o * o, hsum) * (1.0 / HEAD_DIM)
            out_ref[r] = o * lax.rsqrt(ms_o + RMS_EPS) * ng_ref[...] * _silu(x[:, 3 * G_WIDTH:])
        else:
            out_ref[r] = o_ref[r]

    _interleave([row(r) for r in range(u_ref.shape[0])], ROW_LAG)


def _mlstm_kernel(rev, nt, u_ref, gt_ref, ng_ref, *rest):
    if rev:
        hf_ref, out_ref, c_ref, m_ref, h_ref = rest
    else:
        out_ref, c_ref, m_ref, h_ref = rest
    t = T_BLK
    d = 1 if rev else 0
    i = pl.program_id(1)
    heads = range(N_HEADS)
    order = _chunk_order(t, rev)
    hsl = [slice(HEAD_DIM * h, HEAD_DIM * (h + 1)) for h in heads]
    n_chunks = t // CHUNK
    assert 2 * n_chunks == HALO and t % LANES == 0

    @pl.when(i == 0)
    def _():
        c_ref[...] = jnp.zeros_like(c_ref)
        m_ref[...] = jnp.zeros_like(m_ref)

    def row(r):
        x = u_ref[r]
        q = x[:, :G_WIDTH]
        k = x[:, G_WIDTH:2 * G_WIDTH] * (HEAD_DIM ** -0.5)
        v = x[:, 2 * G_WIDTH:3 * G_WIDTH]
        ig_all = gt_ref[r]
        lf_all = ig_all
        tri, _ = _chunk_masks(t, rev)
        tri_bf = jnp.where(tri, 1.0, 0.0).astype(BF16)
        b_all = _dot_sel_lhs(tri_bf, lf_all, n=2)
        b_rows = b_all.T
        ig_rows = ig_all.T

        ig_al = pltpu.roll(ig_all, LANE_GF - LANE_GI, axis=1)
        m_run = m_ref[r, 0:1, :]
        m_in, dec_rows, sc_tiles = {}, {}, {}
        for c in order:
            sl = slice(CHUNK * c, CHUNK * (c + 1))
            rl = CHUNK * c if rev else CHUNK * (c + 1) - 1
            b_last = b_all[rl:rl + 1, :]
            w_end = b_last - b_all[sl] + ig_al[sl]
            m_new = jnp.maximum(b_last + m_run, jnp.max(w_end, axis=0, keepdims=True))
            m_in[c] = m_run
            dec_rows[c] = jnp.exp(b_last + m_run - m_new)
            sc_tiles[c] = jnp.exp(w_end - m_new)
            m_run = m_new
        m_ref[r] = jnp.broadcast_to(m_run, m_ref.shape[1:])
        yield

        lfs = [LANE_GF + N_HEADS * d + h for h in heads]
        sel_r = lax.broadcasted_iota(jnp.int32, (LANES, N_HEADS * LANES), 0)
        sel_c = lax.broadcasted_iota(jnp.int32, (LANES, N_HEADS * LANES), 1)
        sel = jnp.where(sel_r == LANE_GF + N_HEADS * d + (sel_c >> LANE_SHIFT), 1.0, 0.0).astype(BF16)
        b_rep = _dot_sel_rhs(b_all, sel, n=2)
        small_rep = _dot_sel_rhs(
            jnp.concatenate([m_in[c] for c in range(n_chunks)] + [dec_rows[c] for c in range(n_chunks)],
                            axis=0), sel, n=3)
        khs = [k[:, hsl[h]] for h in heads]
        ones_blk = jnp.ones((t, HEAD_DIM), F32)
        v_exts = [jnp.concatenate([v[:, hsl[h]], ones_blk], axis=1) for h in heads]
        intras, s_inters, floors = [], [], []
        for h in heads:
            li = LANE_GI + N_HEADS * d + h
            lf = lfs[h]
            hl = slice(LANES * h, LANES * (h + 1))
            b_h = b_rep[:, hl]
            inter = jnp.concatenate([b_h[CHUNK * c:CHUNK * (c + 1)] + small_rep[c:c + 1, hl]
                                     for c in range(n_chunks)], axis=0)
            d_log = jnp.where(tri, jnp.tile(b_h, (1, t // LANES)) - b_rows[lf:lf + 1, :]
                              + ig_rows[li:li + 1, :], -jnp.inf)
            m_t = jnp.maximum(inter, jnp.max(d_log, axis=1, keepdims=True))
            p = _bdot_nt(q[:, hsl[h]], khs[h]) * jnp.exp(d_log - jnp.tile(m_t, (1, t // LANES)))
            intras.append(_bdot(p, v_exts[h]))
            s_inters.append(jnp.exp(inter - m_t))
            floors.append(jnp.exp(-m_t))
            if h % 2 == 1:
                yield
        incs, decs = {}, {}
        for c in order:
            sl = slice(CHUNK * c, CHUNK * (c + 1))
            for h in heads:
                lf = lfs[h]
                decs[h, c] = small_rep[n_chunks + c:n_chunks + c + 1, LANES * h:LANES * (h + 1)]
                incs[h, c] = _bdot_tn(sc_tiles[c][:, lf:lf + 1] * khs[h][sl], v_exts[h][sl])
        yield

        cns = [c_ref[N_HEADS * r + h] for h in heads]
        for c in order:
            sl = slice(CHUNK * c, CHUNK * (c + 1))
            for h in heads:
                nd = intras[h][sl] + s_inters[h][sl] * _bdot(q[sl, hsl[h]], cns[h])
                den = jnp.maximum(jnp.abs(pltpu.roll(nd, HEAD_DIM, axis=1)), floors[h][sl])
                h_ref[r, sl, hsl[h]] = (nd / den)[:, :HEAD_DIM]
                cns[h] = decs[h, c] * cns[h] + incs[h, c]
            yield
        for h in heads:
            c_ref[N_HEADS * r + h] = cns[h]

        if rev:
            hh = hf_ref[r] + h_ref[r]
            ms = _dot_sel_rhs(hh * hh, _head_sum_matrix(G_WIDTH)) * (1.0 / HEAD_DIM)
            out_ref[r] = hh * lax.rsqrt(ms + RMS_EPS) * ng_ref[...] * _sigmoid(x[:, 3 * G_WIDTH:])
        else:
            out_ref[r] = h_ref[r]

    _interleave([row(r) for r in range(u_ref.shape[0])], ROW_LAG)


def _rwkv_kernel(rev, nt, u_ref, hp_ref, hn_ref, mu_ref, wup_ref, w0_ref, a0_ref, gup_ref,
                 kk_ref, ka_ref, rk_ref, gnw_ref, gnb_ref, *rest):
    if rev:
        yf_ref, out_ref, xe_ref, s_ref, y_ref = rest
    else:
        out_ref, xe_ref, s_ref, y_ref = rest
    t = T_BLK
    i = pl.program_id(1)
    blk = nt - 1 - i if rev else i
    heads = range(N_HEADS)
    hsl = [slice(HEAD_DIM * h, HEAD_DIM * (h + 1)) for h in heads]
    order = _chunk_order(t, rev)

    @pl.when(i == 0)
    def _():
        s_ref[...] = jnp.zeros_like(s_ref)

    def row(r):
        x = u_ref[r]
        _fill_halo(xe_ref.at[r], hp_ref[r], hn_ref[r], x, blk, nt, t)
        x_prev = xe_ref[r, pl.ds(HALO - 1, t), :]
        x_next = xe_ref[r, pl.ds(HALO + 1, t), :]
        us = x + mu_ref[0:1, :] * (x_prev - x) + mu_ref[1:2, :] * (x_next - x)
        yield
        rr = us[:, :G_WIDTH]
        k = us[:, G_WIDTH:2 * G_WIDTH]
        v = us[:, 2 * G_WIDTH:3 * G_WIDTH]
        low = us[:, 3 * G_WIDTH:3 * G_WIDTH + RWKV_RANK_COLS]
        lane = lax.broadcasted_iota(jnp.int32, (t, RWKV_RANK_COLS), 1)
        low = jnp.where(lane < RWKV_RANK_COLS // 2, jnp.tanh(low), low)
        za = _bdot(low, wup_ref[...])
        z_w = w0_ref[...] + za[:, :G_WIDTH]
        lw = -jnp.exp(-_softplus(-z_w) - 0.5)
        a = _sigmoid(a0_ref[...] + za[:, G_WIDTH:])
        gate = _bdot(_sigmoid(us[:, 3 * G_WIDTH + RWKV_RANK_COLS:]), gup_ref[...])
        hsum = _head_sum_matrix(G_WIDTH)
        kk = k * kk_ref[...]
        kk = kk * lax.rsqrt(_dot_sel_rhs(kk * kk, hsum) + L2_EPS)
        k_dir = k * (1.0 + (a - 1.0) * ka_ref[...])
        akk = a * kk
        yield

        tri, strict = _chunk_masks(t, rev)
        tri_bf = jnp.where(tri, 1.0, 0.0).astype(BF16)
        gc = _dot_sel_lhs(tri_bf, lw, n=2)
        e_inc = jnp.exp(gc)
        e_exc = jnp.exp(gc - lw)
        e_neg = jnp.exp(-gc)
        left_a = -kk * e_exc
        left_r = rr * e_inc
        right_b = akk * e_neg
        right_k = k_dir * e_neg
        yield

        a_abs, a_rbs, rhss, yvs = [], [], [], []
        for h in heads:
            hs = hsl[h]
            gram = _bdot_nt(jnp.concatenate([left_a[:, hs], left_r[:, hs]], axis=0),
                            jnp.concatenate([right_b[:, hs], right_k[:, hs]], axis=0))
            a_abs.append(jnp.where(strict, gram[:t, :t], 0.0))
            a_rbs.append(jnp.where(tri, gram[t:, :t], 0.0))
            a_k = jnp.concatenate([jnp.where(strict, gram[:t, t:], 0.0),
                                   jnp.where(tri, gram[t:, t:], 0.0)], axis=0)
            akv = _bdot(a_k, v[:, hs])
            rhss.append(jnp.concatenate([left_a[:, hs], akv[:t]], axis=1))
            yvs.append(akv[t:])
            if h % 2 == 1:
                yield
        sols = yield from _solve_nilpotent(a_abs, rhss)
        yield

        terms = {}
        for c in order:
            sl = slice(CHUNK * c, CHUNK * (c + 1))
            rl = CHUNK * c if rev else CHUNK * (c + 1) - 1
            for h in heads:
                hs = hsl[h]
                sol_c = sols[h][sl]
                g_last = gc[rl:rl + 1, hs]
                tail = jnp.exp(g_last - gc[sl, hs])
                bd = akk[sl, hs] * tail
                kd = k_dir[sl, hs] * tail
                ar = _bdot(a_rbs[h][sl, sl], sol_c)
                y_lhs = left_r[sl, hs] + ar[:, :HEAD_DIM]
                y_const = ar[:, HEAD_DIM:] + yvs[h][sl]
                phi = _bdot_tn(sol_c[:, :HEAD_DIM], bd)
                s_const = _bdot_tn(jnp.concatenate([sol_c[:, HEAD_DIM:], v[sl, hs]], axis=0),
                                   jnp.concatenate([bd, kd], axis=0))
                terms[h, c] = (y_lhs, y_const, phi, s_const, jnp.exp(g_last))
        yield

        states = [s_ref[N_HEADS * r + h] for h in heads]
        for c in order:
            sl = slice(CHUNK * c, CHUNK * (c + 1))
            for h in heads:
                y_lhs, y_const, phi, s_const, e_last = terms[h, c]
                y_ref[r, sl, hsl[h]] = _bdot_nt(y_lhs, states[h]) + y_const
                states[h] = states[h] * e_last + _bdot(states[h], phi) + s_const
            yield
        for h in heads:
            s_ref[N_HEADS * r + h] = states[h]

        if rev:
            y = yf_ref[r] + y_ref[r]
            mean = _dot_sel_rhs(y, hsum) * (1.0 / HEAD_DIM)
            dlt = y - mean
            var = _dot_sel_rhs(dlt * dlt, hsum) * (1.0 / HEAD_DIM)
            yn = dlt * lax.rsqrt(var + RWKV_GN_EPS) * gnw_ref[...] + gnb_ref[...]
            bonus = _dot_sel_rhs(rr * k * rk_ref[...], hsum) * v
            out_ref[r] = (yn + bonus) * gate
        else:
            out_ref[r] = y_ref[r]

    _interleave([row(r) for r in range(u_ref.shape[0])], ROW_LAG)


def _mixer_call(body, rev, rows, t, u3, col_blk, col_w, halo_w, use_gates, params, prev_out, scratch, name):
    bsz, seq, _ = u3.shape
    nt = seq // t
    halo_per_blk = t // HALO

    def blk_of(i):
        return nt - 1 - i if rev else i

    in_specs = [pl.BlockSpec((rows, t, col_w), lambda b, i: (b, blk_of(i), col_blk))]
    args = [u3]
    if halo_w:
        halo_blk = col_blk * col_w // halo_w
        in_specs += [
            pl.BlockSpec((rows, HALO, halo_w),
                         lambda b, i: (b, jnp.maximum(blk_of(i) * halo_per_blk - 1, 0), halo_blk)),
            pl.BlockSpec((rows, HALO, halo_w),
                         lambda b, i: (b, jnp.minimum((blk_of(i) + 1) * halo_per_blk, seq // HALO - 1),
                                       halo_blk)),
        ]
        args += [u3, u3]
    if use_gates:
        in_specs.append(pl.BlockSpec((rows, t, LANES), lambda b, i: (b, blk_of(i), GATE_TILE_BLK)))
        args.append(u3)
    for p in params:
        in_specs.append(pl.BlockSpec(p.shape, lambda b, i, nd=p.ndim: (0,) * nd))
        args.append(p)
    if rev:
        in_specs.append(pl.BlockSpec((rows, t, G_WIDTH), lambda b, i: (b, blk_of(i), 0)))
        args.append(prev_out)
    return pl.pallas_call(
        functools.partial(body, rev, nt),
        out_shape=jax.ShapeDtypeStruct((bsz, seq, G_WIDTH), F32),
        grid=(bsz // rows, nt),
        in_specs=in_specs,
        out_specs=pl.BlockSpec((rows, t, G_WIDTH), lambda b, i: (b, blk_of(i), 0)),
        scratch_shapes=scratch,
        compiler_params=pltpu.CompilerParams(dimension_semantics=("arbitrary", "arbitrary"),
                                             vmem_limit_bytes=VMEM_LIMIT),
        name=name + ("_bwd" if rev else "_fwd"),
    )(*args)


def _both_dirs(body, rows, t, u3, col_blk, col_w, halo_w, use_gates, params_of_dir, scratch, name):
    fwd = _mixer_call(body, False, rows, t, u3, col_blk, col_w, halo_w, use_gates, params_of_dir(0), None,
                      scratch, name)
    return _mixer_call(body, True, rows, t, u3, col_blk, col_w, halo_w, use_gates, params_of_dir(1), fwd,
                       scratch, name)


def _row(v):
    return v.reshape(1, -1).astype(F32)


def _gate_row(values, lane0):
    flat = values.reshape(-1).astype(F32)
    return jnp.zeros((1, LANES), F32).at[0, lane0:lane0 + flat.shape[0]].set(flat)


def _lru_mixer(u3, conv_w, conv_b, gate_w, gate_b, lam):
    t = min(T_LRU, u3.shape[1])
    eye = jnp.eye(N_HEADS, dtype=F32)

    def params(d):
        dense = jnp.einsum('gnio,nm->nigmo', gate_w[d], eye).reshape(G_WIDTH, 2 * G_WIDTH)
        return [conv_w, _row(conv_b), dense.astype(BF16), _row(gate_b[d]), _row(lam[d])]

    scratch = [pltpu.VMEM((1, t + 2 * HALO, G_WIDTH), F32), pltpu.VMEM((t, G_WIDTH), F32),
               pltpu.VMEM((t, G_WIDTH), F32), pltpu.VMEM((t, G_WIDTH), F32),
               pltpu.VMEM((1, G_WIDTH), F32)]
    return _both_dirs(_lru_kernel, 1, t, u3, OFF_LRU // (2 * G_WIDTH), 2 * G_WIDTH, G_WIDTH, False,
                      params, scratch, "rglru")


def _gdn_mixer(u3, conv_w, norm_g):
    t, rows = T_BLK, ROWS_CHUNKED
    shared = [conv_w, _row(jnp.tile(norm_g, N_HEADS))]
    scratch = [pltpu.VMEM((rows, t + 2 * HALO, 3 * G_WIDTH), F32),
               pltpu.VMEM((rows * N_HEADS, HEAD_DIM, HEAD_DIM), F32), pltpu.VMEM((rows, t, G_WIDTH), F32)]
    return _both_dirs(_gdn_kernel, rows, t, u3, OFF_GDN // (4 * G_WIDTH), 4 * G_WIDTH, 3 * G_WIDTH, True,
                      lambda d: shared, scratch, "gdn")


def _mlstm_mixer(u3, norm_g):
    t, rows = T_BLK, ROWS_CHUNKED
    shared = [_row(norm_g)]
    scratch = [pltpu.VMEM((rows * N_HEADS, HEAD_DIM, 2 * HEAD_DIM), F32),
               pltpu.VMEM((rows, HALO, LANES), F32), pltpu.VMEM((rows, t, G_WIDTH), F32)]
    return _both_dirs(_mlstm_kernel, rows, t, u3, OFF_MLSTM // (4 * G_WIDTH), 4 * G_WIDTH, 0, True,
                      lambda d: shared, scratch, "mlstm")


def _rwkv_mixer(u3, mu, w0, w_up, a0, a_up, g_up, k_k, k_a, r_k, gn_w, gn_b):
    t, rows = T_BLK, ROWS_CHUNKED
    rank = w_up.shape[1]
    mu_pad = jnp.pad(mu, ((0, 0), (0, 4 * G_WIDTH - mu.shape[1])))
    gup_pad = jnp.pad(g_up, ((0, LANES - g_up.shape[0]), (0, 0))).astype(BF16)

    def params(d):
        wup = jnp.zeros((RWKV_RANK_COLS, 2 * G_WIDTH), F32)
        wup = wup.at[rank * d:rank * (d + 1), :G_WIDTH].set(w_up[d])
        half = RWKV_RANK_COLS // 2
        wup = wup.at[half + rank * d:half + rank * (d + 1), G_WIDTH:].set(a_up[d])
        return [mu_pad, wup.astype(BF16), _row(w0[d]), _row(a0[d]), gup_pad, _row(k_k), _row(k_a),
                _row(r_k), _row(gn_w), _row(gn_b)]

    scratch = [pltpu.VMEM((rows, t + 2 * HALO, 4 * G_WIDTH), F32),
               pltpu.VMEM((rows * N_HEADS, HEAD_DIM, HEAD_DIM), F32), pltpu.VMEM((rows, t, G_WIDTH), F32)]
    return _both_dirs(_rwkv_kernel, rows, t, u3, OFF_RWKV // (4 * G_WIDTH), 4 * G_WIDTH, 4 * G_WIDTH, False,
                      params, scratch, "rwkv")


def _outffn_kernel(n_ff_chunks, x_ref, ya_ref, yb_ref, yc_ref, yd_ref, wout_ref, gpost_ref, gpre_ref,
                   wfi_ref, wfo_ref, gfpost_ref, o_ref):
    y = jnp.concatenate([ya_ref[...], yb_ref[...], yc_ref[...], yd_ref[...]], axis=-1)
    mixed = jnp.dot(y.astype(BF16), wout_ref[...], preferred_element_type=F32)
    x1 = x_ref[...] + _rms(mixed, gpost_ref[...])
    h = _rms(x1, gpre_ref[...]).astype(BF16)
    d_ff = wfo_ref.shape[0]
    fc = d_ff // n_ff_chunks
    acc = None
    for j in range(n_ff_chunks):
        gate = jnp.dot(h, wfi_ref[:, fc * j:fc * (j + 1)], preferred_element_type=F32)
        up = jnp.dot(h, wfi_ref[:, d_ff + fc * j:d_ff + fc * (j + 1)], preferred_element_type=F32)
        part = jnp.dot((_silu(gate) * up).astype(BF16), wfo_ref[fc * j:fc * (j + 1), :],
                       preferred_element_type=F32)
        acc = part if acc is None else acc + part
    o_ref[...] = x1 + _rms(acc, gfpost_ref[...])


def _outffn(x2, ys, w_out, g_post, g_pre, w_fi, w_fo, g_fpost):
    n, d = x2.shape
    d_ff = w_fo.shape[0]
    n_ff_chunks = 2
    assert (d_ff // n_ff_chunks) % LANES == 0
    tile = lambda w: pl.BlockSpec((TM_PROJ, w), lambda i: (i, 0))
    const = lambda a: pl.BlockSpec(a.shape, lambda i: (0, 0), pipeline_mode=pl.Buffered(1))
    params = [w_out, g_post, g_pre, w_fi, w_fo, g_fpost]
    return pl.pallas_call(
        functools.partial(_outffn_kernel, n_ff_chunks),
        out_shape=jax.ShapeDtypeStruct((n, d), F32),
        grid=(n // TM_PROJ,),
        in_specs=[tile(d)] + [tile(G_WIDTH)] * 4 + [const(p) for p in params],
        out_specs=tile(d),
        compiler_params=pltpu.CompilerParams(dimension_semantics=("arbitrary",),
                                             vmem_limit_bytes=VMEM_LIMIT),
        name="out_ffn",
    )(x2, *ys, *params)


def _permute_w_in(w_in):
    g = G_WIDTH
    a0, b0 = 0, 2 * g
    c0 = b0 + 4 * g + 2 * N_DIR * N_HEADS
    d0 = c0 + 4 * g + 2 * N_DIR * N_HEADS
    d_cols = w_in.shape[1] - d0
    pieces = [w_in[:, b0:b0 + 4 * g], w_in[:, c0:c0 + 4 * g], w_in[:, d0:],
              w_in[:, b0 + 4 * g:c0], w_in[:, c0 + 4 * g:d0]]
    used = 8 * g + d_cols + 4 * N_DIR * N_HEADS
    pieces.append(jnp.zeros((w_in.shape[0], OFF_LRU - used), w_in.dtype))
    pieces.append(w_in[:, a0:a0 + 2 * g])
    return jnp.concatenate(pieces, axis=1)


def _layer(x, n_mix_pre, n_mix_post, n_ffn_pre, n_ffn_post, w_in, w_out,
           lru_conv_w, lru_conv_b, lru_gate_w, lru_gate_b, lru_lambda,
           gdn_conv_w, gdn_a_log, gdn_dt_bias, gdn_norm, mlstm_gate_bias, mlstm_norm,
           rwkv_mu, rwkv_w0, rwkv_w_up, rwkv_a0, rwkv_a_up, rwkv_g_up,
           rwkv_k_k, rwkv_k_a, rwkv_r_k, rwkv_gn_w, rwkv_gn_b, ffn_w_in, ffn_w_out):
    bsz, seq, d = x.shape
    x2 = x.reshape(bsz * seq, d)
    gate_bias_row = (_gate_row(mlstm_gate_bias[:, 0], LANE_GI) + _gate_row(mlstm_gate_bias[:, 1], LANE_GF))
    u3 = _inproj(x2, _row(n_mix_pre), _permute_w_in(w_in).astype(BF16),
                 _gate_row(gdn_a_log, LANE_ALPHA), _gate_row(gdn_dt_bias, LANE_ALPHA),
                 gate_bias_row).reshape(bsz, seq, P_PAD)
    ys = [
        _lru_mixer(u3, lru_conv_w, lru_conv_b, lru_gate_w, lru_gate_b, lru_lambda),
        _gdn_mixer(u3, gdn_conv_w, gdn_norm),
        _mlstm_mixer(u3, mlstm_norm),
        _rwkv_mixer(u3, rwkv_mu, rwkv_w0, rwkv_w_up, rwkv_a0, rwkv_a_up, rwkv_g_up,
                    rwkv_k_k, rwkv_k_a, rwkv_r_k, rwkv_gn_w, rwkv_gn_b),
    ]
    ys = [y.reshape(bsz * seq, G_WIDTH) for y in ys]
    out = _outffn(x2, ys, w_out.astype(BF16), _row(n_mix_post), _row(n_ffn_pre),
                  ffn_w_in.astype(BF16), ffn_w_out.astype(BF16), _row(n_ffn_post))
    return out.reshape(bsz, seq, d)


def kernel(x, norm_mix_pre, norm_mix_post, norm_ffn_pre, norm_ffn_post, w_in, w_out, lru_conv_w, lru_conv_b, lru_gate_w, lru_gate_b, lru_lambda, gdn_conv_w, gdn_a_log, gdn_dt_bias, gdn_norm, mlstm_gate_bias, mlstm_norm, rwkv_mu, rwkv_w0, rwkv_w_up, rwkv_a0, rwkv_a_up, rwkv_g_up, rwkv_k_k, rwkv_k_a, rwkv_r_k, rwkv_gn_w, rwkv_gn_b, ffn_w_in, ffn_w_out):
    bsz, seq, d = x.shape
    assert d == N_HEADS * G_WIDTH and seq % T_BLK == 0 and (bsz * seq) % TM_PROJ == 0
    assert bsz % ROWS_CHUNKED == 0
    assert w_in.shape[-1] - (10 * G_WIDTH + 4 * N_DIR * N_HEADS) == 4 * G_WIDTH - RWKV_G_RANK
    stacked = (norm_mix_pre, norm_mix_post, norm_ffn_pre, norm_ffn_post, w_in, w_out,
               lru_conv_w, lru_conv_b, lru_gate_w, lru_gate_b, lru_lambda,
               gdn_conv_w, gdn_a_log, gdn_dt_bias, gdn_norm, mlstm_gate_bias, mlstm_norm,
               rwkv_mu, rwkv_w0, rwkv_w_up, rwkv_a0, rwkv_a_up, rwkv_g_up,
               rwkv_k_k, rwkv_k_a, rwkv_r_k, rwkv_gn_w, rwkv_gn_b, ffn_w_in, ffn_w_out)
    for l in range(w_in.shape[0]):
        x = _layer(x, *(p[l] for p in stacked))
    return x
```

```python
import functools

import jax
import jax.numpy as jnp
from jax import lax
from jax.experimental import pallas as pl
from jax.experimental.pallas import tpu as pltpu

F32 = jnp.float32
BF16 = jnp.bfloat16

HEAD_DIM = 64
N_HEADS = 4
G_WIDTH = N_HEADS * HEAD_DIM
N_DIR = 2
CONV_K = 4
CHUNK = 64
CHUNK_SHIFT = 6
LRU_C = 8.0
GATE_CAP = 15.0
RWKV_RANK_COLS = 128
RWKV_G_RANK = 64
RWKV_GN_EPS = 64e-5
RMS_EPS = 1e-6
L2_EPS = 1e-6

LANES = 128
LANE_SHIFT = 7
HALO = 8
BLK_GDN = (128, 4)
BLK_MLSTM = (256, 2)
BLK_RWKV = (128, 4)
T_LRU = 1024
ROW_LAG = 1
TM_PROJ = 512
VMEM_LIMIT = 56 * 1024 * 1024

OFF_GDN = 0
OFF_MLSTM = 4 * G_WIDTH
OFF_RWKV = 8 * G_WIDTH
OFF_LRU = 12 * G_WIDTH
P_PAD = 14 * G_WIDTH
GATE_TILE_BLK = (OFF_LRU - LANES) // LANES
LANE_ALPHA = RWKV_G_RANK
LANE_BETA = LANE_ALPHA + N_DIR * N_HEADS
LANE_GI = LANE_BETA + N_DIR * N_HEADS
LANE_GF = LANE_GI + N_DIR * N_HEADS


def _bdot(a, b):
    return jnp.dot(a.astype(BF16), b.astype(BF16), preferred_element_type=F32)


def _bdot_nt(a, b):
    return lax.dot_general(a.astype(BF16), b.astype(BF16), (((1,), (1,)), ((), ())),
                           preferred_element_type=F32)


def _bdot_tn(a, b):
    return lax.dot_general(a.astype(BF16), b.astype(BF16), (((0,), (0,)), ((), ())),
                           preferred_element_type=F32)


def _split_bf16(x, n):
    parts, r = [], x
    for _ in range(n):
        p = r.astype(BF16)
        parts.append(p)
        r = r - p.astype(F32)
    return parts


def _dot_sel_lhs(m_bf16, x, n=3):
    acc = None
    for p in _split_bf16(x, n):
        t = jnp.dot(m_bf16, p, preferred_element_type=F32)
        acc = t if acc is None else acc + t
    return acc


def _dot_sel_rhs(x, m_bf16, n=2):
    acc = None
    for p in _split_bf16(x, n):
        t = jnp.dot(p, m_bf16, preferred_element_type=F32)
        acc = t if acc is None else acc + t
    return acc


def _softplus(x):
    return jnp.maximum(x, 0.0) + jnp.log1p(jnp.exp(-jnp.abs(x)))


def _sigmoid(x):
    return 1.0 / (1.0 + jnp.exp(-x))


def _silu(x):
    return x * _sigmoid(x)


def _gelu_tanh(x):
    c = 0.7978845608028654
    return x * (0.5 * (1.0 + jnp.tanh(c * (x + 0.044715 * (x * x * x)))))


def _soft_cap(x):
    return GATE_CAP * jnp.tanh(x / GATE_CAP)


def _rms(x, g):
    return x * lax.rsqrt(jnp.mean(x * x, axis=-1, keepdims=True) + RMS_EPS) * g


def _head_sum_matrix(width):
    i = lax.broadcasted_iota(jnp.int32, (width, width), 0)
    j = lax.broadcasted_iota(jnp.int32, (width, width), 1)
    return jnp.where((i >> CHUNK_SHIFT) == (j >> CHUNK_SHIFT), 1.0, 0.0).astype(BF16)


def _chunk_masks(t, rev):
    i = lax.broadcasted_iota(jnp.int32, (t, t), 0)
    j = lax.broadcasted_iota(jnp.int32, (t, t), 1)
    same = (i >> CHUNK_SHIFT) == (j >> CHUNK_SHIFT)
    if rev:
        return same & (i <= j), same & (i < j)
    return same & (i >= j), same & (i > j)


def _solve_nilpotent(ms, rhss):
    xs = [r + _bdot(m, r) for m, r in zip(ms, rhss)]
    ps = list(ms)
    for _ in range(CHUNK_SHIFT - 1):
        yield
        ps = [_bdot(p, p) for p in ps]
        xs = [x + _bdot(p, x) for p, x in zip(ps, xs)]
    return xs


def _interleave(programs, lag):
    done = [False] * len(programs)
    step = 0
    while not all(done):
        for k, prog in enumerate(programs):
            if done[k] or step < lag * k:
                continue
            try:
                next(prog)
            except StopIteration:
                done[k] = True
        step += 1


def _chunk_order(t, rev):
    n = t // CHUNK
    return list(range(n - 1, -1, -1)) if rev else list(range(n))


def _inproj_kernel(tiles_per_seq, x_ref, xp_ref, xn_ref, g_ref, w_ref, alog_ref, dtb_ref, bias_ref,
                   gcw_ref, lcw_ref, lcb_ref, mu_ref, u_ref, ue_ref):
    i = pl.program_id(0)
    tm = x_ref.shape[0]
    pos = lax.rem(i, tiles_per_seq)
    xs = jnp.concatenate([jnp.where(pos == 0, 0.0, xp_ref[...]), x_ref[...],
                          jnp.where(pos == tiles_per_seq - 1, 0.0, xn_ref[...])], axis=0)
    h = _rms(xs, g_ref[...]).astype(BF16)
    body = pl.ds(HALO, tm)

    def project(c0, c1):
        ue_ref[:, c0:c1] = jnp.dot(h, w_ref[:, c0:c1], preferred_element_type=F32)

    def conv(c0, c1, cw_ref):
        acc = None
        for j in range(CONV_K):
            term = cw_ref[j:j + 1, :] * ue_ref[pl.ds(HALO - CONV_K // 2 + j, tm), c0:c1]
            acc = term if acc is None else acc + term
        return acc

    project(OFF_GDN, OFF_MLSTM)
    project(OFF_RWKV, OFF_LRU)
    u_ref[:, OFF_GDN:OFF_GDN + 3 * G_WIDTH] = _silu(conv(OFF_GDN, OFF_GDN + 3 * G_WIDTH, gcw_ref))
    u_ref[:, OFF_GDN + 3 * G_WIDTH:OFF_MLSTM] = ue_ref[body, OFF_GDN + 3 * G_WIDTH:OFF_MLSTM]
    project(OFF_LRU, P_PAD)
    cur = ue_ref[body, OFF_RWKV:OFF_LRU]
    prv = ue_ref[pl.ds(HALO - 1, tm), OFF_RWKV:OFF_LRU]
    nxt = ue_ref[pl.ds(HALO + 1, tm), OFF_RWKV:OFF_LRU]
    u_ref[:, OFF_RWKV:OFF_LRU] = cur + mu_ref[0:1, :] * (prv - cur) + mu_ref[1:2, :] * (nxt - cur)
    c0 = GATE_TILE_BLK * LANES
    raw = ue_ref[body, c0:c0 + LANES]
    lane = lax.broadcasted_iota(jnp.int32, raw.shape, 1)
    capped = _soft_cap(raw + bias_ref[...])
    done = jnp.where(lane < LANE_BETA, -jnp.exp(alog_ref[...]) * _softplus(raw + dtb_ref[...]),
                     jnp.where(lane < LANE_GI, _sigmoid(raw),
                               jnp.where(lane < LANE_GF, capped, -_softplus(-capped))))
    u_ref[:, c0:c0 + LANES] = jnp.where(lane >= LANE_ALPHA, done, u_ref[:, c0:c0 + LANES])
    u_ref[:, OFF_MLSTM:OFF_RWKV] = jnp.dot(h[HALO:HALO + tm], w_ref[:, OFF_MLSTM:OFF_RWKV],
                                           preferred_element_type=F32)
    u_ref[:, OFF_LRU:OFF_LRU + G_WIDTH] = conv(OFF_LRU, OFF_LRU + G_WIDTH, lcw_ref) + lcb_ref[...]
    u_ref[:, OFF_LRU + G_WIDTH:] = ue_ref[body, OFF_LRU + G_WIDTH:]


def _inproj(x2, seq, g, w, alog_row, dtb_row, bias_row, gdn_cw, lru_cw, lru_cb, mu_pad):
    n, d = x2.shape
    p = w.shape[1]
    per_tile = TM_PROJ // HALO
    const = lambda a: pl.BlockSpec(a.shape, lambda i: (0, 0))
    params = [g, w, alog_row, dtb_row, bias_row, gdn_cw, lru_cw, lru_cb, mu_pad]
    return pl.pallas_call(
        functools.partial(_inproj_kernel, seq // TM_PROJ),
        out_shape=jax.ShapeDtypeStruct((n, p), F32),
        grid=(n // TM_PROJ,),
        in_specs=[pl.BlockSpec((TM_PROJ, d), lambda i: (i, 0)),
                  pl.BlockSpec((HALO, d), lambda i: (jnp.maximum(i * per_tile - 1, 0), 0)),
                  pl.BlockSpec((HALO, d), lambda i: (jnp.minimum((i + 1) * per_tile, n // HALO - 1), 0))]
                 + [const(a) for a in params],
        out_specs=pl.BlockSpec((TM_PROJ, p), lambda i: (i, 0)),
        scratch_shapes=[pltpu.VMEM((TM_PROJ + 2 * HALO, p), F32)],
        compiler_params=pltpu.CompilerParams(dimension_semantics=("arbitrary",),
                                             vmem_limit_bytes=VMEM_LIMIT),
        name="in_proj",
    )(x2, x2, x2, *params)


def _lru_kernel(rev, nt, u_ref, wg_ref, gb_ref, lam_ref, *rest):
    if rev:
        hf_ref, out_ref, a_ref, b_ref, h_ref, carry_ref = rest
    else:
        out_ref, a_ref, b_ref, h_ref, carry_ref = rest
    t = u_ref.shape[1]
    i = pl.program_id(1)
    x = u_ref[0]
    xc = x[:, :G_WIDTH]
    pre = _bdot(xc, wg_ref[...]) + gb_ref[...]
    r = _sigmoid(pre[:, :G_WIDTH])
    ig = _sigmoid(pre[:, G_WIDTH:])
    log_a = -LRU_C * r * _softplus(-lam_ref[...])
    a = jnp.exp(log_a)
    th = jnp.tanh(log_a)
    b = jnp.sqrt(-2.0 * th / (1.0 - th)) * ig * xc

    pos = lax.broadcasted_iota(jnp.int32, (t, G_WIDTH), 0) & (HALO - 1)
    shift = 1
    while shift < HALO:
        if rev:
            keep = pos < HALO - shift
            a_sh = jnp.where(keep, pltpu.roll(a, t - shift, axis=0), 1.0)
            b_sh = jnp.where(keep, pltpu.roll(b, t - shift, axis=0), 0.0)
        else:
            keep = pos >= shift
            a_sh = jnp.where(keep, pltpu.roll(a, shift, axis=0), 1.0)
            b_sh = jnp.where(keep, pltpu.roll(b, shift, axis=0), 0.0)
        b = a * b_sh + b
        a = a * a_sh
        shift *= 2
    a_ref[...] = a
    b_ref[...] = b

    @pl.when(i == 0)
    def _():
        carry_ref[...] = jnp.zeros_like(carry_ref)

    n_groups = t // HALO

    def step(s, h):
        g = n_groups - 1 - s if rev else s
        rows = pl.ds(pl.multiple_of(g * HALO, HALO), HALO)
        h_grp = a_ref[rows, :] * h + b_ref[rows, :]
        h_ref[rows, :] = h_grp
        return h_grp[0:1, :] if rev else h_grp[HALO - 1:HALO, :]

    carry_ref[...] = lax.fori_loop(0, n_groups, step, carry_ref[...], unroll=4)
    if rev:
        out_ref[0] = (hf_ref[0] + h_ref[...]) * _gelu_tanh(x[:, G_WIDTH:])
    else:
        out_ref[0] = h_ref[...]


def _gdn_kernel(rev, nt, u_ref, gt_ref, ng_ref, *rest):
    if rev:
        of_ref, out_ref, s_ref, o_ref = rest
    else:
        out_ref, s_ref, o_ref = rest
    t = u_ref.shape[1]
    d = 1 if rev else 0
    i = pl.program_id(1)
    heads = range(N_HEADS)
    hsl = [slice(HEAD_DIM * h, HEAD_DIM * (h + 1)) for h in heads]
    order = _chunk_order(t, rev)

    @pl.when(i == 0)
    def _():
        s_ref[...] = jnp.zeros_like(s_ref)

    def row(r):
        x = u_ref[r]
        hsum = _head_sum_matrix(G_WIDTH)
        q = x[:, :G_WIDTH]
        k = x[:, G_WIDTH:2 * G_WIDTH]
        v = x[:, 2 * G_WIDTH:3 * G_WIDTH]
        q = q * lax.rsqrt(_dot_sel_rhs(q * q, hsum) + L2_EPS) * (HEAD_DIM ** -0.5)
        k = k * lax.rsqrt(_dot_sel_rhs(k * k, hsum) + L2_EPS)
        g_all = gt_ref[r]
        beta_all = g_all
        tri, strict = _chunk_masks(t, rev)
        tri_bf = jnp.where(tri, 1.0, 0.0).astype(BF16)
        gc_all = _dot_sel_lhs(tri_bf, g_all, n=2)
        gc_rows = gc_all.T
        yield

        gc_cs = [gc_all[:, LANE_ALPHA + N_HEADS * d + h:LANE_ALPHA + N_HEADS * d + h + 1] for h in heads]
        ms, rhss, qks, qes = [], [], [], []
        for h in heads:
            la = LANE_ALPHA + N_HEADS * d + h
            lb = LANE_BETA + N_HEADS * d + h
            beta = beta_all[:, lb:lb + 1]
            dec = jnp.exp(jnp.where(tri, gc_cs[h] - gc_rows[la:la + 1, :], -jnp.inf))
            qh, kh, vh = q[:, hsl[h]], k[:, hsl[h]], v[:, hsl[h]]
            kb = kh * beta
            e_gc = jnp.exp(gc_cs[h])
            ms.append(jnp.where(strict, -(_bdot_nt(kb, kh) * dec), 0.0))
            rhss.append(jnp.concatenate([vh * beta, kb * e_gc], axis=1))
            qks.append(_bdot_nt(qh, kh) * dec)
            qes.append(qh * e_gc)
        yield
        sols = yield from _solve_nilpotent(ms, rhss)
        yield

        terms = {}
        for c in order:
            sl = slice(CHUNK * c, CHUNK * (c + 1))
            rl = CHUNK * c if rev else CHUNK * (c + 1) - 1
            for h in heads:
                la = LANE_ALPHA + N_HEADS * d + h
                g_last = jnp.broadcast_to(gc_all[rl:rl + 1, :], (CHUNK, LANES))[:, la:la + 1]
                kd = k[sl, hsl[h]] * jnp.exp(g_last - gc_cs[h][sl])
                kuw = _bdot_tn(kd, sols[h][sl])
                quw = _bdot(qks[h][sl, sl], sols[h][sl])
                lhs = jnp.concatenate([qes[h][sl] - quw[:, HEAD_DIM:], -kuw[:, HEAD_DIM:]], axis=0)
                terms[h, c] = (lhs, quw[:, :HEAD_DIM], kuw[:, :HEAD_DIM], jnp.exp(g_last))
        yield

        states = [s_ref[N_HEADS * r + h] for h in heads]
        for c in order:
            sl = slice(CHUNK * c, CHUNK * (c + 1))
            for h in heads:
                lhs, o_const, s_const, e_last = terms[h, c]
                both = _bdot(lhs, states[h])
                o_ref[r, sl, hsl[h]] = both[:CHUNK] + o_const
                states[h] = e_last * states[h] + both[CHUNK:] + s_const
            yield
        for h in heads:
            s_ref[N_HEADS * r + h] = states[h]

        if rev:
            o = of_ref[r] + o_ref[r]
            ms_o = _dot_sel_rhs(o * o, hsum) * (1.0 / HEAD_DIM)
            out_ref[r] = o * lax.rsqrt(ms_o + RMS_EPS) * ng_ref[...] * _silu(x[:, 3 * G_WIDTH:])
        else:
            out_ref[r] = o_ref[r]

    _interleave([row(r) for r in range(u_ref.shape[0])], ROW_LAG)


def _mlstm_kernel(rev, nt, u_ref, gt_ref, ng_ref, *rest):
    if rev:
        hf_ref, out_ref, c_ref, m_ref, h_ref = rest
    else:
        out_ref, c_ref, m_ref, h_ref = rest
    t = u_ref.shape[1]
    d = 1 if rev else 0
    i = pl.program_id(1)
    heads = range(N_HEADS)
    order = _chunk_order(t, rev)
    hsl = [slice(HEAD_DIM * h, HEAD_DIM * (h + 1)) for h in heads]
    n_chunks = t // CHUNK
    assert 2 * n_chunks <= HALO and t % LANES == 0

    @pl.when(i == 0)
    def _():
        c_ref[...] = jnp.zeros_like(c_ref)
        m_ref[...] = jnp.zeros_like(m_ref)

    def row(r):
        x = u_ref[r]
        q = x[:, :G_WIDTH]
        k = x[:, G_WIDTH:2 * G_WIDTH] * (HEAD_DIM ** -0.5)
        v = x[:, 2 * G_WIDTH:3 * G_WIDTH]
        ig_all = gt_ref[r]
        lf_all = ig_all
        tri, _ = _chunk_masks(t, rev)
        tri_bf = jnp.where(tri, 1.0, 0.0).astype(BF16)
        b_all = _dot_sel_lhs(tri_bf, lf_all, n=2)
        b_rows = b_all.T
        ig_rows = ig_all.T

        ig_al = pltpu.roll(ig_all, LANE_GF - LANE_GI, axis=1)
        m_run = m_ref[r, 0:1, :]
        m_in, dec_rows, sc_tiles = {}, {}, {}
        for c in order:
            sl = slice(CHUNK * c, CHUNK * (c + 1))
            rl = CHUNK * c if rev else CHUNK * (c + 1) - 1
            b_last = b_all[rl:rl + 1, :]
            w_end = b_last - b_all[sl] + ig_al[sl]
            m_new = jnp.maximum(b_last + m_run, jnp.max(w_end, axis=0, keepdims=True))
            m_in[c] = m_run
            dec_rows[c] = jnp.exp(b_last + m_run - m_new)
            sc_tiles[c] = jnp.exp(w_end - m_new)
            m_run = m_new
        m_ref[r] = jnp.broadcast_to(m_run, m_ref.shape[1:])
        yield

        lfs = [LANE_GF + N_HEADS * d + h for h in heads]
        sel_r = lax.broadcasted_iota(jnp.int32, (LANES, N_HEADS * LANES), 0)
        sel_c = lax.broadcasted_iota(jnp.int32, (LANES, N_HEADS * LANES), 1)
        sel = jnp.where(sel_r == LANE_GF + N_HEADS * d + (sel_c >> LANE_SHIFT), 1.0, 0.0).astype(BF16)
        b_rep = _dot_sel_rhs(b_all, sel, n=2)
        small_rep = _dot_sel_rhs(
            jnp.concatenate([m_in[c] for c in range(n_chunks)] + [dec_rows[c] for c in range(n_chunks)]
                            + [jnp.zeros((HALO - 2 * n_chunks, LANES), F32)] * (2 * n_chunks < HALO),
                            axis=0), sel, n=3)
        khs = [k[:, hsl[h]] for h in heads]
        ones_blk = jnp.ones((t, HEAD_DIM), F32)
        v_exts = [jnp.concatenate([v[:, hsl[h]], ones_blk], axis=1) for h in heads]
        intras, s_inters, floors = [], [], []
        for h in heads:
            li = LANE_GI + N_HEADS * d + h
            lf = lfs[h]
            hl = slice(LANES * h, LANES * (h + 1))
            b_h = b_rep[:, hl]
            inter = jnp.concatenate([b_h[CHUNK * c:CHUNK * (c + 1)] + small_rep[c:c + 1, hl]
                                     for c in range(n_chunks)], axis=0)
            d_log = jnp.where(tri, jnp.tile(b_h, (1, t // LANES)) - b_rows[lf:lf + 1, :]
                              + ig_rows[li:li + 1, :], -jnp.inf)
            m_t = jnp.maximum(inter, jnp.max(d_log, axis=1, keepdims=True))
            p = _bdot_nt(q[:, hsl[h]], khs[h]) * jnp.exp(d_log - jnp.tile(m_t, (1, t // LANES)))
            intras.append(_bdot(p, v_exts[h]))
            s_inters.append(jnp.exp(inter - m_t))
            floors.append(jnp.exp(-m_t))
            if h % 2 == 1:
                yield
        incs, decs = {}, {}
        for c in order:
            sl = slice(CHUNK * c, CHUNK * (c + 1))
            for h in heads:
                lf = lfs[h]
                decs[h, c] = small_rep[n_chunks + c:n_chunks + c + 1, LANES * h:LANES * (h + 1)]
                incs[h, c] = _bdot_tn(sc_tiles[c][:, lf:lf + 1] * khs[h][sl], v_exts[h][sl])
        yield

        cns = [c_ref[N_HEADS * r + h] for h in heads]
        for c in order:
            sl = slice(CHUNK * c, CHUNK * (c + 1))
            for h in heads:
                nd = intras[h][sl] + s_inters[h][sl] * _bdot(q[sl, hsl[h]], cns[h])
                den = jnp.maximum(jnp.abs(pltpu.roll(nd, HEAD_DIM, axis=1)), floors[h][sl])
                h_ref[r, sl, hsl[h]] = (nd / den)[:, :HEAD_DIM]
                cns[h] = decs[h, c] * cns[h] + incs[h, c]
            yield
        for h in heads:
            c_ref[N_HEADS * r + h] = cns[h]

        if rev:
            hh = hf_ref[r] + h_ref[r]
            ms = _dot_sel_rhs(hh * hh, _head_sum_matrix(G_WIDTH)) * (1.0 / HEAD_DIM)
            out_ref[r] = hh * lax.rsqrt(ms + RMS_EPS) * ng_ref[...] * _sigmoid(x[:, 3 * G_WIDTH:])
        else:
            out_ref[r] = h_ref[r]

    _interleave([row(r) for r in range(u_ref.shape[0])], ROW_LAG)


def _rwkv_kernel(rev, nt, u_ref, wup_ref, w0_ref, a0_ref, gup_ref,
                 kk_ref, ka_ref, rk_ref, gnw_ref, gnb_ref, *rest):
    if rev:
        yf_ref, out_ref, s_ref, y_ref = rest
    else:
        out_ref, s_ref, y_ref = rest
    t = u_ref.shape[1]
    i = pl.program_id(1)
    heads = range(N_HEADS)
    hsl = [slice(HEAD_DIM * h, HEAD_DIM * (h + 1)) for h in heads]
    order = _chunk_order(t, rev)

    @pl.when(i == 0)
    def _():
        s_ref[...] = jnp.zeros_like(s_ref)

    def row(r):
        us = u_ref[r]
        rr = us[:, :G_WIDTH]
        k = us[:, G_WIDTH:2 * G_WIDTH]
        v = us[:, 2 * G_WIDTH:3 * G_WIDTH]
        low = us[:, 3 * G_WIDTH:3 * G_WIDTH + RWKV_RANK_COLS]
        lane = lax.broadcasted_iota(jnp.int32, (t, RWKV_RANK_COLS), 1)
        low = jnp.where(lane < RWKV_RANK_COLS // 2, jnp.tanh(low), low)
        za = _bdot(low, wup_ref[...])
        z_w = w0_ref[...] + za[:, :G_WIDTH]
        lw = -jnp.exp(-_softplus(-z_w) - 0.5)
        a = _sigmoid(a0_ref[...] + za[:, G_WIDTH:])
        gate = _bdot(_sigmoid(us[:, 3 * G_WIDTH + RWKV_RANK_COLS:]), gup_ref[...])
        hsum = _head_sum_matrix(G_WIDTH)
        kk = k * kk_ref[...]
        kk = kk * lax.rsqrt(_dot_sel_rhs(kk * kk, hsum) + L2_EPS)
        k_dir = k * (1.0 + (a - 1.0) * ka_ref[...])
        akk = a * kk
        yield

        tri, strict = _chunk_masks(t, rev)
        tri_bf = jnp.where(tri, 1.0, 0.0).astype(BF16)
        gc = _dot_sel_lhs(tri_bf, lw, n=2)
        e_inc = jnp.exp(gc)
        e_exc = jnp.exp(gc - lw)
        e_neg = jnp.exp(-gc)
        left_a = -kk * e_exc
        left_r = rr * e_inc
        right_b = akk * e_neg
        right_k = k_dir * e_neg
        yield

        a_abs, a_rbs, rhss, yvs = [], [], [], []
        for h in heads:
            hs = hsl[h]
            gram = _bdot_nt(jnp.concatenate([left_a[:, hs], left_r[:, hs]], axis=0),
                            jnp.concatenate([right_b[:, hs], right_k[:, hs]], axis=0))
            a_abs.append(jnp.where(strict, gram[:t, :t], 0.0))
            a_rbs.append(jnp.where(tri, gram[t:, :t], 0.0))
            a_k = jnp.concatenate([jnp.where(strict, gram[:t, t:], 0.0),
                                   jnp.where(tri, gram[t:, t:], 0.0)], axis=0)
            akv = _bdot(a_k, v[:, hs])
            rhss.append(jnp.concatenate([left_a[:, hs], akv[:t]], axis=1))
            yvs.append(akv[t:])
            if h % 2 == 1:
                yield
        sols = yield from _solve_nilpotent(a_abs, rhss)
        yield

        terms = {}
        for c in order:
            sl = slice(CHUNK * c, CHUNK * (c + 1))
            rl = CHUNK * c if rev else CHUNK * (c + 1) - 1
            for h in heads:
                hs = hsl[h]
                sol_c = sols[h][sl]
                g_last = gc[rl:rl + 1, hs]
                tail = jnp.exp(g_last - gc[sl, hs])
                bd = akk[sl, hs] * tail
                kd = k_dir[sl, hs] * tail
                ar = _bdot(a_rbs[h][sl, sl], sol_c)
                y_lhs = left_r[sl, hs] + ar[:, :HEAD_DIM]
                y_const = ar[:, HEAD_DIM:] + yvs[h][sl]
                phi = _bdot_tn(sol_c[:, :HEAD_DIM], bd)
                s_const = _bdot_tn(jnp.concatenate([sol_c[:, HEAD_DIM:], v[sl, hs]], axis=0),
                                   jnp.concatenate([bd, kd], axis=0))
                terms[h, c] = (y_lhs, y_const, phi, s_const, jnp.exp(g_last))
        yield

        states = [s_ref[N_HEADS * r + h] for h in heads]
        for c in order:
            sl = slice(CHUNK * c, CHUNK * (c + 1))
            for h in heads:
                y_lhs, y_const, phi, s_const, e_last = terms[h, c]
                y_ref[r, sl, hsl[h]] = _bdot_nt(y_lhs, states[h]) + y_const
                states[h] = states[h] * e_last + _bdot(states[h], phi) + s_const
            yield
        for h in heads:
            s_ref[N_HEADS * r + h] = states[h]

        if rev:
            y = yf_ref[r] + y_ref[r]
            mean = _dot_sel_rhs(y, hsum) * (1.0 / HEAD_DIM)
            dlt = y - mean
            var = _dot_sel_rhs(dlt * dlt, hsum) * (1.0 / HEAD_DIM)
            yn = dlt * lax.rsqrt(var + RWKV_GN_EPS) * gnw_ref[...] + gnb_ref[...]
            bonus = _dot_sel_rhs(rr * k * rk_ref[...], hsum) * v
            out_ref[r] = (yn + bonus) * gate
        else:
            out_ref[r] = y_ref[r]

    _interleave([row(r) for r in range(u_ref.shape[0])], ROW_LAG)


def _mixer_call(body, rev, rows, t, u3, col_blk, col_w, use_gates, params, prev_out, scratch, name):
    bsz, seq, _ = u3.shape
    nt = seq // t

    def blk_of(i):
        return nt - 1 - i if rev else i

    in_specs = [pl.BlockSpec((rows, t, col_w), lambda b, i: (b, blk_of(i), col_blk))]
    args = [u3]
    if use_gates:
        in_specs.append(pl.BlockSpec((rows, t, LANES), lambda b, i: (b, blk_of(i), GATE_TILE_BLK)))
        args.append(u3)
    for p in params:
        in_specs.append(pl.BlockSpec(p.shape, lambda b, i, nd=p.ndim: (0,) * nd))
        args.append(p)
    if rev:
        in_specs.append(pl.BlockSpec((rows, t, G_WIDTH), lambda b, i: (b, blk_of(i), 0)))
        args.append(prev_out)
    return pl.pallas_call(
        functools.partial(body, rev, nt),
        out_shape=jax.ShapeDtypeStruct((bsz, seq, G_WIDTH), F32),
        grid=(bsz // rows, nt),
        in_specs=in_specs,
        out_specs=pl.BlockSpec((rows, t, G_WIDTH), lambda b, i: (b, blk_of(i), 0)),
        scratch_shapes=scratch,
        compiler_params=pltpu.CompilerParams(dimension_semantics=("arbitrary", "arbitrary"),
                                             vmem_limit_bytes=VMEM_LIMIT),
        name=name + ("_bwd" if rev else "_fwd"),
    )(*args)


def _both_dirs(body, rows, t, u3, col_blk, col_w, use_gates, params_of_dir, scratch, name):
    fwd = _mixer_call(body, False, rows, t, u3, col_blk, col_w, use_gates, params_of_dir(0), None,
                      scratch, name)
    return _mixer_call(body, True, rows, t, u3, col_blk, col_w, use_gates, params_of_dir(1), fwd,
                       scratch, name)


def _row(v):
    return v.reshape(1, -1).astype(F32)


def _gate_row(values, lane0):
    flat = values.reshape(-1).astype(F32)
    return jnp.zeros((1, LANES), F32).at[0, lane0:lane0 + flat.shape[0]].set(flat)


def _lru_mixer(u3, gate_w, gate_b, lam):
    t = min(T_LRU, u3.shape[1])
    eye = jnp.eye(N_HEADS, dtype=F32)

    def params(d):
        dense = jnp.einsum('gnio,nm->nigmo', gate_w[d], eye).reshape(G_WIDTH, 2 * G_WIDTH)
        return [dense.astype(BF16), _row(gate_b[d]), _row(lam[d])]

    scratch = [pltpu.VMEM((t, G_WIDTH), F32), pltpu.VMEM((t, G_WIDTH), F32), pltpu.VMEM((t, G_WIDTH), F32),
               pltpu.VMEM((1, G_WIDTH), F32)]
    return _both_dirs(_lru_kernel, 1, t, u3, OFF_LRU // (2 * G_WIDTH), 2 * G_WIDTH, False,
                      params, scratch, "rglru")


def _gdn_mixer(u3, norm_g):
    t, rows = BLK_GDN
    shared = [_row(jnp.tile(norm_g, N_HEADS))]
    scratch = [pltpu.VMEM((rows * N_HEADS, HEAD_DIM, HEAD_DIM), F32), pltpu.VMEM((rows, t, G_WIDTH), F32)]
    return _both_dirs(_gdn_kernel, rows, t, u3, OFF_GDN // (4 * G_WIDTH), 4 * G_WIDTH, True,
                      lambda d: shared, scratch, "gdn")


def _mlstm_mixer(u3, norm_g):
    t, rows = BLK_MLSTM
    shared = [_row(norm_g)]
    scratch = [pltpu.VMEM((rows * N_HEADS, HEAD_DIM, 2 * HEAD_DIM), F32),
               pltpu.VMEM((rows, HALO, LANES), F32), pltpu.VMEM((rows, t, G_WIDTH), F32)]
    return _both_dirs(_mlstm_kernel, rows, t, u3, OFF_MLSTM // (4 * G_WIDTH), 4 * G_WIDTH, True,
                      lambda d: shared, scratch, "mlstm")


def _rwkv_mixer(u3, w0, w_up, a0, a_up, g_up, k_k, k_a, r_k, gn_w, gn_b):
    t, rows = BLK_RWKV
    rank = w_up.shape[1]
    gup_pad = jnp.pad(g_up, ((0, LANES - g_up.shape[0]), (0, 0))).astype(BF16)

    def params(d):
        wup = jnp.zeros((RWKV_RANK_COLS, 2 * G_WIDTH), F32)
        wup = wup.at[rank * d:rank * (d + 1), :G_WIDTH].set(w_up[d])
        half = RWKV_RANK_COLS // 2
        wup = wup.at[half + rank * d:half + rank * (d + 1), G_WIDTH:].set(a_up[d])
        return [wup.astype(BF16), _row(w0[d]), _row(a0[d]), gup_pad, _row(k_k), _row(k_a),
                _row(r_k), _row(gn_w), _row(gn_b)]

    scratch = [pltpu.VMEM((rows * N_HEADS, HEAD_DIM, HEAD_DIM), F32), pltpu.VMEM((rows, t, G_WIDTH), F32)]
    return _both_dirs(_rwkv_kernel, rows, t, u3, OFF_RWKV // (4 * G_WIDTH), 4 * G_WIDTH, False,
                      params, scratch, "rwkv")


def _outffn_kernel(n_ff_chunks, x_ref, ya_ref, yb_ref, yc_ref, yd_ref, wout_ref, gpost_ref, gpre_ref,
                   wfi_ref, wfo_ref, gfpost_ref, o_ref):
    y = jnp.concatenate([ya_ref[...], yb_ref[...], yc_ref[...], yd_ref[...]], axis=-1)
    mixed = jnp.dot(y.astype(BF16), wout_ref[...], preferred_element_type=F32)
    x1 = x_ref[...] + _rms(mixed, gpost_ref[...])
    h = _rms(x1, gpre_ref[...]).astype(BF16)
    d_ff = wfo_ref.shape[0]
    fc = d_ff // n_ff_chunks
    acc = None
    for j in range(n_ff_chunks):
        gate = jnp.dot(h, wfi_ref[:, fc * j:fc * (j + 1)], preferred_element_type=F32)
        up = jnp.dot(h, wfi_ref[:, d_ff + fc * j:d_ff + fc * (j + 1)], preferred_element_type=F32)
        part = jnp.dot((_silu(gate) * up).astype(BF16), wfo_ref[fc * j:fc * (j + 1), :],
                       preferred_element_type=F32)
        acc = part if acc is None else acc + part
    o_ref[...] = x1 + _rms(acc, gfpost_ref[...])


def _outffn(x2, ys, w_out, g_post, g_pre, w_fi, w_fo, g_fpost):
    n, d = x2.shape
    d_ff = w_fo.shape[0]
    n_ff_chunks = 2
    assert (d_ff // n_ff_chunks) % LANES == 0
    tile = lambda w: pl.BlockSpec((TM_PROJ, w), lambda i: (i, 0))
    const = lambda a: pl.BlockSpec(a.shape, lambda i: (0, 0), pipeline_mode=pl.Buffered(1))
    params = [w_out, g_post, g_pre, w_fi, w_fo, g_fpost]
    return pl.pallas_call(
        functools.partial(_outffn_kernel, n_ff_chunks),
        out_shape=jax.ShapeDtypeStruct((n, d), F32),
        grid=(n // TM_PROJ,),
        in_specs=[tile(d)] + [tile(G_WIDTH)] * 4 + [const(p) for p in params],
        out_specs=tile(d),
        compiler_params=pltpu.CompilerParams(dimension_semantics=("arbitrary",),
                                             vmem_limit_bytes=VMEM_LIMIT),
        name="out_ffn",
    )(x2, *ys, *params)


def _permute_w_in(w_in):
    g = G_WIDTH
    a0, b0 = 0, 2 * g
    c0 = b0 + 4 * g + 2 * N_DIR * N_HEADS
    d0 = c0 + 4 * g + 2 * N_DIR * N_HEADS
    d_cols = w_in.shape[1] - d0
    pieces = [w_in[:, b0:b0 + 4 * g], w_in[:, c0:c0 + 4 * g], w_in[:, d0:],
              w_in[:, b0 + 4 * g:c0], w_in[:, c0 + 4 * g:d0]]
    used = 8 * g + d_cols + 4 * N_DIR * N_HEADS
    pieces.append(jnp.zeros((w_in.shape[0], OFF_LRU - used), w_in.dtype))
    pieces.append(w_in[:, a0:a0 + 2 * g])
    return jnp.concatenate(pieces, axis=1)


def _layer(x, n_mix_pre, n_mix_post, n_ffn_pre, n_ffn_post, w_in, w_out,
           lru_conv_w, lru_conv_b, lru_gate_w, lru_gate_b, lru_lambda,
           gdn_conv_w, gdn_a_log, gdn_dt_bias, gdn_norm, mlstm_gate_bias, mlstm_norm,
           rwkv_mu, rwkv_w0, rwkv_w_up, rwkv_a0, rwkv_a_up, rwkv_g_up,
           rwkv_k_k, rwkv_k_a, rwkv_r_k, rwkv_gn_w, rwkv_gn_b, ffn_w_in, ffn_w_out):
    bsz, seq, d = x.shape
    x2 = x.reshape(bsz * seq, d)
    gate_bias_row = (_gate_row(mlstm_gate_bias[:, 0], LANE_GI) + _gate_row(mlstm_gate_bias[:, 1], LANE_GF))
    mu_pad = jnp.pad(rwkv_mu, ((0, 0), (0, OFF_LRU - OFF_RWKV - rwkv_mu.shape[1])))
    u3 = _inproj(x2, seq, _row(n_mix_pre), _permute_w_in(w_in).astype(BF16),
                 _gate_row(gdn_a_log, LANE_ALPHA), _gate_row(gdn_dt_bias, LANE_ALPHA), gate_bias_row,
                 gdn_conv_w, lru_conv_w, _row(lru_conv_b), mu_pad).reshape(bsz, seq, P_PAD)
    ys = [
        _lru_mixer(u3, lru_gate_w, lru_gate_b, lru_lambda),
        _gdn_mixer(u3, gdn_norm),
        _mlstm_mixer(u3, mlstm_norm),
        _rwkv_mixer(u3, rwkv_w0, rwkv_w_up, rwkv_a0, rwkv_a_up, rwkv_g_up,
                    rwkv_k_k, rwkv_k_a, rwkv_r_k, rwkv_gn_w, rwkv_gn_b),
    ]
    ys = [y.reshape(bsz * seq, G_WIDTH) for y in ys]
    out = _outffn(x2, ys, w_out.astype(BF16), _row(n_mix_post), _row(n_ffn_pre),
                  ffn_w_in.astype(BF16), ffn_w_out.astype(BF16), _row(n_ffn_post))
    return out.reshape(bsz, seq, d)


def kernel(x, norm_mix_pre, norm_mix_post, norm_ffn_pre, norm_ffn_post, w_in, w_out, lru_conv_w, lru_conv_b, lru_gate_w, lru_gate_b, lru_lambda, gdn_conv_w, gdn_a_log, gdn_dt_bias, gdn_norm, mlstm_gate_bias, mlstm_norm, rwkv_mu, rwkv_w0, rwkv_w_up, rwkv_a0, rwkv_a_up, rwkv_g_up, rwkv_k_k, rwkv_k_a, rwkv_r_k, rwkv_gn_w, rwkv_gn_b, ffn_w_in, ffn_w_out):
    bsz, seq, d = x.shape
    assert d == N_HEADS * G_WIDTH and seq % TM_PROJ == 0
    assert all(seq % t == 0 and bsz % rows == 0 for t, rows in (BLK_GDN, BLK_MLSTM, BLK_RWKV))
    assert seq % min(T_LRU, seq) == 0
    assert w_in.shape[-1] - (10 * G_WIDTH + 4 * N_DIR * N_HEADS) == 4 * G_WIDTH - RWKV_G_RANK
    stacked = (norm_mix_pre, norm_mix_post, norm_ffn_pre, norm_ffn_post, w_in, w_out,
               lru_conv_w, lru_conv_b, lru_gate_w, lru_gate_b, lru_lambda,
               gdn_conv_w, gdn_a_log, gdn_dt_bias, gdn_norm, mlstm_gate_bias, mlstm_norm,
               rwkv_mu, rwkv_w0, rwkv_w_up, rwkv_a0, rwkv_a_up, rwkv_g_up,
               rwkv_k_k, rwkv_k_a, rwkv_r_k, rwkv_gn_w, rwkv_gn_b, ffn_w_in, ffn_w_out)
    for l in range(w_in.shape[0]):
        x = _layer(x, *(p[l] for p in stacked))
    return x
```

```python
import functools

import jax
import jax.numpy as jnp
from jax import lax
from jax.experimental import pallas as pl
from jax.experimental.pallas import tpu as pltpu

F32 = jnp.float32
BF16 = jnp.bfloat16

HEAD_DIM = 64
N_HEADS = 4
G_WIDTH = N_HEADS * HEAD_DIM
N_DIR = 2
CONV_K = 4
CHUNK = 64
CHUNK_SHIFT = 6
LRU_C = 8.0
GATE_CAP = 15.0
RWKV_RANK_COLS = 128
RWKV_G_RANK = 64
RWKV_GN_EPS = 64e-5
RMS_EPS = 1e-6
L2_EPS = 1e-6

LANES = 128
LANE_SHIFT = 7
HALO = 8
BLK_GDN = (128, 4)
BLK_MLSTM = (256, 2)
BLK_RWKV = (128, 4)
T_LRU = 1024
ROW_LAG = 1
TM_PROJ = 512
VMEM_LIMIT = 56 * 1024 * 1024

OFF_GDN = 0
OFF_MLSTM = 4 * G_WIDTH
OFF_RWKV = 8 * G_WIDTH
OFF_LRU = 12 * G_WIDTH
P_PAD = 14 * G_WIDTH
GATE_TILE_BLK = (OFF_LRU - LANES) // LANES
LANE_ALPHA = RWKV_G_RANK
LANE_BETA = LANE_ALPHA + N_DIR * N_HEADS
LANE_GI = LANE_BETA + N_DIR * N_HEADS
LANE_GF = LANE_GI + N_DIR * N_HEADS


def _bdot(a, b):
    return jnp.dot(a.astype(BF16), b.astype(BF16), preferred_element_type=F32)


def _bdot_nt(a, b):
    return lax.dot_general(a.astype(BF16), b.astype(BF16), (((1,), (1,)), ((), ())),
                           preferred_element_type=F32)


def _bdot_tn(a, b):
    return lax.dot_general(a.astype(BF16), b.astype(BF16), (((0,), (0,)), ((), ())),
                           preferred_element_type=F32)


def _split_bf16(x, n):
    parts, r = [], x
    for _ in range(n):
        p = r.astype(BF16)
        parts.append(p)
        r = r - p.astype(F32)
    return parts


def _dot_sel_lhs(m_bf16, x, n=3):
    acc = None
    for p in _split_bf16(x, n):
        t = jnp.dot(m_bf16, p, preferred_element_type=F32)
        acc = t if acc is None else acc + t
    return acc


def _dot_sel_rhs(x, m_bf16, n=2):
    acc = None
    for p in _split_bf16(x, n):
        t = jnp.dot(p, m_bf16, preferred_element_type=F32)
        acc = t if acc is None else acc + t
    return acc


def _softplus(x):
    return jnp.maximum(x, 0.0) + jnp.log1p(jnp.exp(-jnp.abs(x)))


def _sigmoid(x):
    return 1.0 / (1.0 + jnp.exp(-x))


def _silu(x):
    return x * _sigmoid(x)


def _gelu_tanh(x):
    c = 0.7978845608028654
    return x * (0.5 * (1.0 + jnp.tanh(c * (x + 0.044715 * (x * x * x)))))


def _soft_cap(x):
    return GATE_CAP * jnp.tanh(x / GATE_CAP)


def _rms(x, g):
    return x * lax.rsqrt(jnp.mean(x * x, axis=-1, keepdims=True) + RMS_EPS) * g


def _head_sum_matrix(width):
    i = lax.broadcasted_iota(jnp.int32, (width, width), 0)
    j = lax.broadcasted_iota(jnp.int32, (width, width), 1)
    return jnp.where((i >> CHUNK_SHIFT) == (j >> CHUNK_SHIFT), 1.0, 0.0).astype(BF16)


def _chunk_masks(t, rev):
    i = lax.broadcasted_iota(jnp.int32, (t, t), 0)
    j = lax.broadcasted_iota(jnp.int32, (t, t), 1)
    same = (i >> CHUNK_SHIFT) == (j >> CHUNK_SHIFT)
    if rev:
        return same & (i <= j), same & (i < j)
    return same & (i >= j), same & (i > j)


def _solve_nilpotent(ms, rhss):
    xs = [r + _bdot(m, r) for m, r in zip(ms, rhss)]
    ps = list(ms)
    for _ in range(CHUNK_SHIFT - 1):
        yield
        ps = [_bdot(p, p) for p in ps]
        xs = [x + _bdot(p, x) for p, x in zip(ps, xs)]
    return xs


def _interleave(programs, lag):
    done = [False] * len(programs)
    step = 0
    while not all(done):
        for k, prog in enumerate(programs):
            if done[k] or step < lag * k:
                continue
            try:
                next(prog)
            except StopIteration:
                done[k] = True
        step += 1


def _chunk_order(t, rev):
    n = t // CHUNK
    return list(range(n - 1, -1, -1)) if rev else list(range(n))


def _inproj_kernel(tiles_per_seq, x_ref, xp_ref, xn_ref, g_ref, w_ref, alog_ref, dtb_ref, bias_ref,
                   gcw_ref, lcw_ref, lcb_ref, mu_ref, u_ref, ue_ref):
    i = pl.program_id(0)
    tm = x_ref.shape[0]
    pos = lax.rem(i, tiles_per_seq)
    xs = jnp.concatenate([jnp.where(pos == 0, 0.0, xp_ref[...]), x_ref[...],
                          jnp.where(pos == tiles_per_seq - 1, 0.0, xn_ref[...])], axis=0)
    h = _rms(xs, g_ref[...]).astype(BF16)
    body = pl.ds(HALO, tm)

    def project(c0, c1):
        if OFF_MLSTM <= c0 < OFF_RWKV:
            u_ref[:, c0:c1] = jnp.dot(h[HALO:HALO + tm], w_ref[:, c0:c1], preferred_element_type=F32)
        else:
            ue_ref[:, c0:c1] = jnp.dot(h, w_ref[:, c0:c1], preferred_element_type=F32)

    def conv(c0, c1, cw_ref, off):
        acc = None
        for j in range(CONV_K):
            term = cw_ref[j:j + 1, c0 - off:c1 - off] * ue_ref[pl.ds(HALO - CONV_K // 2 + j, tm), c0:c1]
            acc = term if acc is None else acc + term
        return acc

    def epilogue(c0, c1):
        if c1 <= OFF_GDN + 3 * G_WIDTH:
            u_ref[:, c0:c1] = _silu(conv(c0, c1, gcw_ref, OFF_GDN))
        elif OFF_RWKV <= c0 < OFF_LRU:
            cur = ue_ref[body, c0:c1]
            prv = ue_ref[pl.ds(HALO - 1, tm), c0:c1]
            nxt = ue_ref[pl.ds(HALO + 1, tm), c0:c1]
            mu0 = mu_ref[0:1, c0 - OFF_RWKV:c1 - OFF_RWKV]
            mu1 = mu_ref[1:2, c0 - OFF_RWKV:c1 - OFF_RWKV]
            us = cur + mu0 * (prv - cur) + mu1 * (nxt - cur)
            if c1 == OFF_LRU:
                raw = us[:, -LANES:]
                lane = lax.broadcasted_iota(jnp.int32, raw.shape, 1)
                capped = _soft_cap(raw + bias_ref[...])
                done = jnp.where(lane < LANE_BETA, -jnp.exp(alog_ref[...]) * _softplus(raw + dtb_ref[...]),
                                 jnp.where(lane < LANE_GI, _sigmoid(raw),
                                           jnp.where(lane < LANE_GF, capped, -_softplus(-capped))))
                us = jnp.concatenate([us[:, :-LANES], jnp.where(lane >= LANE_ALPHA, done, raw)], axis=1)
            u_ref[:, c0:c1] = us
        elif c0 == OFF_LRU:
            u_ref[:, c0:c1] = conv(c0, c1, lcw_ref, OFF_LRU) + lcb_ref[...]
        elif not OFF_MLSTM <= c0 < OFF_RWKV:
            u_ref[:, c0:c1] = ue_ref[body, c0:c1]

    piece = 2 * LANES
    order = [0, 4, 8, 1, 5, 9, 2, 6, 10, 11, 7, 12, 3, 13]
    assert sorted(order) == list(range(u_ref.shape[1] // piece)) and G_WIDTH == piece
    for n, p in enumerate(order):
        project(piece * p, piece * (p + 1))
        if n:
            epilogue(piece * order[n - 1], piece * (order[n - 1] + 1))
    epilogue(piece * order[-1], piece * (order[-1] + 1))


def _inproj(x2, seq, g, w, alog_row, dtb_row, bias_row, gdn_cw, lru_cw, lru_cb, mu_pad):
    n, d = x2.shape
    p = w.shape[1]
    per_tile = TM_PROJ // HALO
    const = lambda a: pl.BlockSpec(a.shape, lambda i: (0, 0))
    params = [g, w, alog_row, dtb_row, bias_row, gdn_cw, lru_cw, lru_cb, mu_pad]
    return pl.pallas_call(
        functools.partial(_inproj_kernel, seq // TM_PROJ),
        out_shape=jax.ShapeDtypeStruct((n, p), F32),
        grid=(n // TM_PROJ,),
        in_specs=[pl.BlockSpec((TM_PROJ, d), lambda i: (i, 0)),
                  pl.BlockSpec((HALO, d), lambda i: (jnp.maximum(i * per_tile - 1, 0), 0)),
                  pl.BlockSpec((HALO, d), lambda i: (jnp.minimum((i + 1) * per_tile, n // HALO - 1), 0))]
                 + [const(a) for a in params],
        out_specs=pl.BlockSpec((TM_PROJ, p), lambda i: (i, 0)),
        scratch_shapes=[pltpu.VMEM((TM_PROJ + 2 * HALO, p), F32)],
        compiler_params=pltpu.CompilerParams(dimension_semantics=("arbitrary",),
                                             vmem_limit_bytes=VMEM_LIMIT),
        name="in_proj",
    )(x2, x2, x2, *params)


def _lru_kernel(rev, nt, u_ref, wg_ref, gb_ref, lam_ref, *rest):
    if rev:
        hf_ref, out_ref, a_ref, b_ref, h_ref, carry_ref = rest
    else:
        out_ref, a_ref, b_ref, h_ref, carry_ref = rest
    t = u_ref.shape[1]
    i = pl.program_id(1)
    x = u_ref[0]
    xc = x[:, :G_WIDTH]
    pre = _bdot(xc, wg_ref[...]) + gb_ref[...]
    r = _sigmoid(pre[:, :G_WIDTH])
    ig = _sigmoid(pre[:, G_WIDTH:])
    log_a = -LRU_C * r * _softplus(-lam_ref[...])
    a = jnp.exp(log_a)
    th = jnp.tanh(log_a)
    b = jnp.sqrt(-2.0 * th / (1.0 - th)) * ig * xc

    pos = lax.broadcasted_iota(jnp.int32, (t, G_WIDTH), 0) & (HALO - 1)
    shift = 1
    while shift < HALO:
        if rev:
            keep = pos < HALO - shift
            a_sh = jnp.where(keep, pltpu.roll(a, t - shift, axis=0), 1.0)
            b_sh = jnp.where(keep, pltpu.roll(b, t - shift, axis=0), 0.0)
        else:
            keep = pos >= shift
            a_sh = jnp.where(keep, pltpu.roll(a, shift, axis=0), 1.0)
            b_sh = jnp.where(keep, pltpu.roll(b, shift, axis=0), 0.0)
        b = a * b_sh + b
        a = a * a_sh
        shift *= 2
    a_ref[...] = a
    b_ref[...] = b

    @pl.when(i == 0)
    def _():
        carry_ref[...] = jnp.zeros_like(carry_ref)

    n_groups = t // HALO

    def step(s, h):
        g = n_groups - 1 - s if rev else s
        rows = pl.ds(pl.multiple_of(g * HALO, HALO), HALO)
        h_grp = a_ref[rows, :] * h + b_ref[rows, :]
        h_ref[rows, :] = h_grp
        return h_grp[0:1, :] if rev else h_grp[HALO - 1:HALO, :]

    carry_ref[...] = lax.fori_loop(0, n_groups, step, carry_ref[...], unroll=4)
    if rev:
        out_ref[0] = (hf_ref[0] + h_ref[...]) * _gelu_tanh(x[:, G_WIDTH:])
    else:
        out_ref[0] = h_ref[...]


def _gdn_kernel(rev, nt, u_ref, gt_ref, ng_ref, *rest):
    if rev:
        of_ref, out_ref, s_ref, o_ref = rest
    else:
        out_ref, s_ref, o_ref = rest
    t = u_ref.shape[1]
    d = 1 if rev else 0
    i = pl.program_id(1)
    heads = range(N_HEADS)
    hsl = [slice(HEAD_DIM * h, HEAD_DIM * (h + 1)) for h in heads]
    order = _chunk_order(t, rev)

    @pl.when(i == 0)
    def _():
        s_ref[...] = jnp.zeros_like(s_ref)

    def row(r):
        x = u_ref[r]
        hsum = _head_sum_matrix(G_WIDTH)
        q = x[:, :G_WIDTH]
        k = x[:, G_WIDTH:2 * G_WIDTH]
        v = x[:, 2 * G_WIDTH:3 * G_WIDTH]
        q = q * lax.rsqrt(_dot_sel_rhs(q * q, hsum) + L2_EPS) * (HEAD_DIM ** -0.5)
        k = k * lax.rsqrt(_dot_sel_rhs(k * k, hsum) + L2_EPS)
        g_all = gt_ref[r]
        beta_all = g_all
        tri, strict = _chunk_masks(t, rev)
        tri_bf = jnp.where(tri, 1.0, 0.0).astype(BF16)
        gc_all = _dot_sel_lhs(tri_bf, g_all, n=2)
        gc_rows = gc_all.T
        yield

        gc_cs = [gc_all[:, LANE_ALPHA + N_HEADS * d + h:LANE_ALPHA + N_HEADS * d + h + 1] for h in heads]
        ms, rhss, qks, qes = [], [], [], []
        for h in heads:
            la = LANE_ALPHA + N_HEADS * d + h
            lb = LANE_BETA + N_HEADS * d + h
            beta = beta_all[:, lb:lb + 1]
            dec = jnp.exp(jnp.where(tri, gc_cs[h] - gc_rows[la:la + 1, :], -jnp.inf))
            qh, kh, vh = q[:, hsl[h]], k[:, hsl[h]], v[:, hsl[h]]
            kb = kh * beta
            e_gc = jnp.exp(gc_cs[h])
            ms.append(jnp.where(strict, -(_bdot_nt(kb, kh) * dec), 0.0))
            rhss.append(jnp.concatenate([vh * beta, kb * e_gc], axis=1))
            qks.append(_bdot_nt(qh, kh) * dec)
            qes.append(qh * e_gc)
        yield
        sols = yield from _solve_nilpotent(ms, rhss)
        yield

        terms = {}
        for c in order:
            sl = slice(CHUNK * c, CHUNK * (c + 1))
            rl = CHUNK * c if rev else CHUNK * (c + 1) - 1
            for h in heads:
                la = LANE_ALPHA + N_HEADS * d + h
                g_last = jnp.broadcast_to(gc_all[rl:rl + 1, :], (CHUNK, LANES))[:, la:la + 1]
                kd = k[sl, hsl[h]] * jnp.exp(g_last - gc_cs[h][sl])
                kuw = _bdot_tn(kd, sols[h][sl])
                quw = _bdot(qks[h][sl, sl], sols[h][sl])
                lhs = jnp.concatenate([qes[h][sl] - quw[:, HEAD_DIM:], -kuw[:, HEAD_DIM:]], axis=0)
                terms[h, c] = (lhs, quw[:, :HEAD_DIM], kuw[:, :HEAD_DIM], jnp.exp(g_last))
        yield

        states = [s_ref[N_HEADS * r + h] for h in heads]
        for c in order:
            sl = slice(CHUNK * c, CHUNK * (c + 1))
            for h in heads:
                lhs, o_const, s_const, e_last = terms[h, c]
                both = _bdot(lhs, states[h])
                o_ref[r, sl, hsl[h]] = both[:CHUNK] + o_const
                states[h] = e_last * states[h] + both[CHUNK:] + s_const
            yield
        for h in heads:
            s_ref[N_HEADS * r + h] = states[h]

        if rev:
            o = of_ref[r] + o_ref[r]
            ms_o = _dot_sel_rhs(o * o, hsum) * (1.0 / HEAD_DIM)
            out_ref[r] = o * lax.rsqrt(ms_o + RMS_EPS) * ng_ref[...] * _silu(x[:, 3 * G_WIDTH:])
        else:
            out_ref[r] = o_ref[r]

    _interleave([row(r) for r in range(u_ref.shape[0])], ROW_LAG)


def _mlstm_kernel(rev, nt, u_ref, gt_ref, ng_ref, *rest):
    if rev:
        hf_ref, out_ref, c_ref, m_ref, h_ref = rest
    else:
        out_ref, c_ref, m_ref, h_ref = rest
    t = u_ref.shape[1]
    d = 1 if rev else 0
    i = pl.program_id(1)
    heads = range(N_HEADS)
    order = _chunk_order(t, rev)
    hsl = [slice(HEAD_DIM * h, HEAD_DIM * (h + 1)) for h in heads]
    n_chunks = t // CHUNK
    assert 2 * n_chunks <= HALO and t % LANES == 0

    @pl.when(i == 0)
    def _():
        c_ref[...] = jnp.zeros_like(c_ref)
        m_ref[...] = jnp.zeros_like(m_ref)

    def row(r):
        x = u_ref[r]
        q = x[:, :G_WIDTH]
        k = x[:, G_WIDTH:2 * G_WIDTH] * (HEAD_DIM ** -0.5)
        v = x[:, 2 * G_WIDTH:3 * G_WIDTH]
        ig_all = gt_ref[r]
        lf_all = ig_all
        tri, _ = _chunk_masks(t, rev)
        tri_bf = jnp.where(tri, 1.0, 0.0).astype(BF16)
        b_all = _dot_sel_lhs(tri_bf, lf_all, n=2)
        b_rows = b_all.T
        ig_rows = ig_all.T

        ig_al = pltpu.roll(ig_all, LANE_GF - LANE_GI, axis=1)
        m_run = m_ref[r, 0:1, :]
        m_in, dec_rows, sc_tiles = {}, {}, {}
        for c in order:
            sl = slice(CHUNK * c, CHUNK * (c + 1))
            rl = CHUNK * c if rev else CHUNK * (c + 1) - 1
            b_last = b_all[rl:rl + 1, :]
            w_end = b_last - b_all[sl] + ig_al[sl]
            m_new = jnp.maximum(b_last + m_run, jnp.max(w_end, axis=0, keepdims=True))
            m_in[c] = m_run
            dec_rows[c] = jnp.exp(b_last + m_run - m_new)
            sc_tiles[c] = jnp.exp(w_end - m_new)
            m_run = m_new
        m_ref[r] = jnp.broadcast_to(m_run, m_ref.shape[1:])
        yield

        lfs = [LANE_GF + N_HEADS * d + h for h in heads]
        sel_r = lax.broadcasted_iota(jnp.int32, (LANES, N_HEADS * LANES), 0)
        sel_c = lax.broadcasted_iota(jnp.int32, (LANES, N_HEADS * LANES), 1)
        sel = jnp.where(sel_r == LANE_GF + N_HEADS * d + (sel_c >> LANE_SHIFT), 1.0, 0.0).astype(BF16)
        b_rep = _dot_sel_rhs(b_all, sel, n=2)
        small_rep = _dot_sel_rhs(
            jnp.concatenate([m_in[c] for c in range(n_chunks)] + [dec_rows[c] for c in range(n_chunks)]
                            + [jnp.zeros((HALO - 2 * n_chunks, LANES), F32)] * (2 * n_chunks < HALO),
                            axis=0), sel, n=3)
        khs = [k[:, hsl[h]] for h in heads]
        ones_blk = jnp.ones((t, HEAD_DIM), F32)
        v_exts = [jnp.concatenate([v[:, hsl[h]], ones_blk], axis=1) for h in heads]
        intras, s_inters, floors = [], [], []
        for h in heads:
            li = LANE_GI + N_HEADS * d + h
            lf = lfs[h]
            hl = slice(LANES * h, LANES * (h + 1))
            b_h = b_rep[:, hl]
            inter = jnp.concatenate([b_h[CHUNK * c:CHUNK * (c + 1)] + small_rep[c:c + 1, hl]
                                     for c in range(n_chunks)], axis=0)
            d_log = jnp.where(tri, jnp.tile(b_h, (1, t // LANES)) - b_rows[lf:lf + 1, :]
                              + ig_rows[li:li + 1, :], -jnp.inf)
            m_t = jnp.maximum(inter, jnp.max(d_log, axis=1, keepdims=True))
            p = _bdot_nt(q[:, hsl[h]], khs[h]) * jnp.exp(d_log - jnp.tile(m_t, (1, t // LANES)))
            intras.append(_bdot(p, v_exts[h]))
            s_inters.append(jnp.exp(inter - m_t))
            floors.append(jnp.exp(-m_t))
            if h % 2 == 1:
                yield
        incs, decs = {}, {}
        for c in order:
            sl = slice(CHUNK * c, CHUNK * (c + 1))
            for h in heads:
                lf = lfs[h]
                decs[h, c] = small_rep[n_chunks + c:n_chunks + c + 1, LANES * h:LANES * (h + 1)]
                incs[h, c] = _bdot_tn(sc_tiles[c][:, lf:lf + 1] * khs[h][sl], v_exts[h][sl])
        yield

        cns = [c_ref[N_HEADS * r + h] for h in heads]
        for c in order:
            sl = slice(CHUNK * c, CHUNK * (c + 1))
            for h in heads:
                nd = intras[h][sl] + s_inters[h][sl] * _bdot(q[sl, hsl[h]], cns[h])
                den = jnp.maximum(jnp.abs(pltpu.roll(nd, HEAD_DIM, axis=1)), floors[h][sl])
                h_ref[r, sl, hsl[h]] = (nd / den)[:, :HEAD_DIM]
                cns[h] = decs[h, c] * cns[h] + incs[h, c]
            yield
        for h in heads:
            c_ref[N_HEADS * r + h] = cns[h]

        if rev:
            hh = hf_ref[r] + h_ref[r]
            ms = _dot_sel_rhs(hh * hh, _head_sum_matrix(G_WIDTH)) * (1.0 / HEAD_DIM)
            out_ref[r] = hh * lax.rsqrt(ms + RMS_EPS) * ng_ref[...] * _sigmoid(x[:, 3 * G_WIDTH:])
        else:
            out_ref[r] = h_ref[r]

    _interleave([row(r) for r in range(u_ref.shape[0])], ROW_LAG)


def _rwkv_kernel(rev, nt, u_ref, wup_ref, w0_ref, a0_ref, gup_ref,
                 kk_ref, ka_ref, rk_ref, gnw_ref, gnb_ref, *rest):
    if rev:
        yf_ref, out_ref, s_ref, y_ref = rest
    else:
        out_ref, s_ref, y_ref = rest
    t = u_ref.shape[1]
    i = pl.program_id(1)
    heads = range(N_HEADS)
    hsl = [slice(HEAD_DIM * h, HEAD_DIM * (h + 1)) for h in heads]
    order = _chunk_order(t, rev)

    @pl.when(i == 0)
    def _():
        s_ref[...] = jnp.zeros_like(s_ref)

    def row(r):
        us = u_ref[r]
        rr = us[:, :G_WIDTH]
        k = us[:, G_WIDTH:2 * G_WIDTH]
        v = us[:, 2 * G_WIDTH:3 * G_WIDTH]
        low = us[:, 3 * G_WIDTH:3 * G_WIDTH + RWKV_RANK_COLS]
        lane = lax.broadcasted_iota(jnp.int32, (t, RWKV_RANK_COLS), 1)
        low = jnp.where(lane < RWKV_RANK_COLS // 2, jnp.tanh(low), low)
        za = _bdot(low, wup_ref[...])
        z_w = w0_ref[...] + za[:, :G_WIDTH]
        lw = -jnp.exp(-_softplus(-z_w) - 0.5)
        a = _sigmoid(a0_ref[...] + za[:, G_WIDTH:])
        gate = _bdot(_sigmoid(us[:, 3 * G_WIDTH + RWKV_RANK_COLS:]), gup_ref[...])
        hsum = _head_sum_matrix(G_WIDTH)
        kk = k * kk_ref[...]
        kk = kk * lax.rsqrt(_dot_sel_rhs(kk * kk, hsum) + L2_EPS)
        k_dir = k * (1.0 + (a - 1.0) * ka_ref[...])
        akk = a * kk
        yield

        tri, strict = _chunk_masks(t, rev)
        tri_bf = jnp.where(tri, 1.0, 0.0).astype(BF16)
        gc = _dot_sel_lhs(tri_bf, lw, n=2)
        e_inc = jnp.exp(gc)
        e_exc = jnp.exp(gc - lw)
        e_neg = jnp.exp(-gc)
        left_a = -kk * e_exc
        left_r = rr * e_inc
        right_b = akk * e_neg
        right_k = k_dir * e_neg
        yield

        a_abs, a_rbs, rhss, yvs = [], [], [], []
        for h in heads:
            hs = hsl[h]
            gram = _bdot_nt(jnp.concatenate([left_a[:, hs], left_r[:, hs]], axis=0),
                            jnp.concatenate([right_b[:, hs], right_k[:, hs]], axis=0))
            a_abs.append(jnp.where(strict, gram[:t, :t], 0.0))
            a_rbs.append(jnp.where(tri, gram[t:, :t], 0.0))
            a_k = jnp.concatenate([jnp.where(strict, gram[:t, t:], 0.0),
                                   jnp.where(tri, gram[t:, t:], 0.0)], axis=0)
            akv = _bdot(a_k, v[:, hs])
            rhss.append(jnp.concatenate([left_a[:, hs], akv[:t]], axis=1))
            yvs.append(akv[t:])
            if h % 2 == 1:
                yield
        sols = yield from _solve_nilpotent(a_abs, rhss)
        yield

        terms = {}
        for c in order:
            sl = slice(CHUNK * c, CHUNK * (c + 1))
            rl = CHUNK * c if rev else CHUNK * (c + 1) - 1
            for h in heads:
                hs = hsl[h]
                sol_c = sols[h][sl]
                g_last = gc[rl:rl + 1, hs]
                tail = jnp.exp(g_last - gc[sl, hs])
                bd = akk[sl, hs] * tail
                kd = k_dir[sl, hs] * tail
                ar = _bdot(a_rbs[h][sl, sl], sol_c)
                y_lhs = left_r[sl, hs] + ar[:, :HEAD_DIM]
                y_const = ar[:, HEAD_DIM:] + yvs[h][sl]
                phi = _bdot_tn(sol_c[:, :HEAD_DIM], bd)
                s_const = _bdot_tn(jnp.concatenate([sol_c[:, HEAD_DIM:], v[sl, hs]], axis=0),
                                   jnp.concatenate([bd, kd], axis=0))
                terms[h, c] = (y_lhs, y_const, phi, s_const, jnp.exp(g_last))
        yield

        states = [s_ref[N_HEADS * r + h] for h in heads]
        for c in order:
            sl = slice(CHUNK * c, CHUNK * (c + 1))
            for h in heads:
                y_lhs, y_const, phi, s_const, e_last = terms[h, c]
                y_ref[r, sl, hsl[h]] = _bdot_nt(y_lhs, states[h]) + y_const
                states[h] = states[h] * e_last + _bdot(states[h], phi) + s_const
            yield
        for h in heads:
            s_ref[N_HEADS * r + h] = states[h]

        if rev:
            y = yf_ref[r] + y_ref[r]
            mean = _dot_sel_rhs(y, hsum) * (1.0 / HEAD_DIM)
            dlt = y - mean
            var = _dot_sel_rhs(dlt * dlt, hsum) * (1.0 / HEAD_DIM)
            yn = dlt * lax.rsqrt(var + RWKV_GN_EPS) * gnw_ref[...] + gnb_ref[...]
            bonus = _dot_sel_rhs(rr * k * rk_ref[...], hsum) * v
            out_ref[r] = (yn + bonus) * gate
        else:
            out_ref[r] = y_ref[r]

    _interleave([row(r) for r in range(u_ref.shape[0])], ROW_LAG)


def _mixer_call(body, rev, rows, t, u3, col_blk, col_w, use_gates, params, prev_out, scratch, name):
    bsz, seq, _ = u3.shape
    nt = seq // t

    def blk_of(i):
        return nt - 1 - i if rev else i

    in_specs = [pl.BlockSpec((rows, t, col_w), lambda b, i: (b, blk_of(i), col_blk))]
    args = [u3]
    if use_gates:
        in_specs.append(pl.BlockSpec((rows, t, LANES), lambda b, i: (b, blk_of(i), GATE_TILE_BLK)))
        args.append(u3)
    for p in params:
        in_specs.append(pl.BlockSpec(p.shape, lambda b, i, nd=p.ndim: (0,) * nd))
        args.append(p)
    if rev:
        in_specs.append(pl.BlockSpec((rows, t, G_WIDTH), lambda b, i: (b, blk_of(i), 0)))
        args.append(prev_out)
    return pl.pallas_call(
        functools.partial(body, rev, nt),
        out_shape=jax.ShapeDtypeStruct((bsz, seq, G_WIDTH), F32),
        grid=(bsz // rows, nt),
        in_specs=in_specs,
        out_specs=pl.BlockSpec((rows, t, G_WIDTH), lambda b, i: (b, blk_of(i), 0)),
        scratch_shapes=scratch,
        compiler_params=pltpu.CompilerParams(dimension_semantics=("arbitrary", "arbitrary"),
                                             vmem_limit_bytes=VMEM_LIMIT),
        name=name + ("_bwd" if rev else "_fwd"),
    )(*args)


def _both_dirs(body, rows, t, u3, col_blk, col_w, use_gates, params_of_dir, scratch, name):
    fwd = _mixer_call(body, False, rows, t, u3, col_blk, col_w, use_gates, params_of_dir(0), None,
                      scratch, name)
    return _mixer_call(body, True, rows, t, u3, col_blk, col_w, use_gates, params_of_dir(1), fwd,
                       scratch, name)


def _row(v):
    return v.reshape(1, -1).astype(F32)


def _gate_row(values, lane0):
    flat = values.reshape(-1).astype(F32)
    return jnp.zeros((1, LANES), F32).at[0, lane0:lane0 + flat.shape[0]].set(flat)


def _lru_mixer(u3, gate_w, gate_b, lam):
    t = min(T_LRU, u3.shape[1])
    eye = jnp.eye(N_HEADS, dtype=F32)

    def params(d):
        dense = jnp.einsum('gnio,nm->nigmo', gate_w[d], eye).reshape(G_WIDTH, 2 * G_WIDTH)
        return [dense.astype(BF16), _row(gate_b[d]), _row(lam[d])]

    scratch = [pltpu.VMEM((t, G_WIDTH), F32), pltpu.VMEM((t, G_WIDTH), F32), pltpu.VMEM((t, G_WIDTH), F32),
               pltpu.VMEM((1, G_WIDTH), F32)]
    return _both_dirs(_lru_kernel, 1, t, u3, OFF_LRU // (2 * G_WIDTH), 2 * G_WIDTH, False,
                      params, scratch, "rglru")


def _gdn_mixer(u3, norm_g):
    t, rows = BLK_GDN
    shared = [_row(jnp.tile(norm_g, N_HEADS))]
    scratch = [pltpu.VMEM((rows * N_HEADS, HEAD_DIM, HEAD_DIM), F32), pltpu.VMEM((rows, t, G_WIDTH), F32)]
    return _both_dirs(_gdn_kernel, rows, t, u3, OFF_GDN // (4 * G_WIDTH), 4 * G_WIDTH, True,
                      lambda d: shared, scratch, "gdn")


def _mlstm_mixer(u3, norm_g):
    t, rows = BLK_MLSTM
    shared = [_row(norm_g)]
    scratch = [pltpu.VMEM((rows * N_HEADS, HEAD_DIM, 2 * HEAD_DIM), F32),
               pltpu.VMEM((rows, HALO, LANES), F32), pltpu.VMEM((rows, t, G_WIDTH), F32)]
    return _both_dirs(_mlstm_kernel, rows, t, u3, OFF_MLSTM // (4 * G_WIDTH), 4 * G_WIDTH, True,
                      lambda d: shared, scratch, "mlstm")


def _rwkv_mixer(u3, w0, w_up, a0, a_up, g_up, k_k, k_a, r_k, gn_w, gn_b):
    t, rows = BLK_RWKV
    rank = w_up.shape[1]
    gup_pad = jnp.pad(g_up, ((0, LANES - g_up.shape[0]), (0, 0))).astype(BF16)

    def params(d):
        wup = jnp.zeros((RWKV_RANK_COLS, 2 * G_WIDTH), F32)
        wup = wup.at[rank * d:rank * (d + 1), :G_WIDTH].set(w_up[d])
        half = RWKV_RANK_COLS // 2
        wup = wup.at[half + rank * d:half + rank * (d + 1), G_WIDTH:].set(a_up[d])
        return [wup.astype(BF16), _row(w0[d]), _row(a0[d]), gup_pad, _row(k_k), _row(k_a),
                _row(r_k), _row(gn_w), _row(gn_b)]

    scratch = [pltpu.VMEM((rows * N_HEADS, HEAD_DIM, HEAD_DIM), F32), pltpu.VMEM((rows, t, G_WIDTH), F32)]
    return _both_dirs(_rwkv_kernel, rows, t, u3, OFF_RWKV // (4 * G_WIDTH), 4 * G_WIDTH, False,
                      params, scratch, "rwkv")


def _outffn_kernel(n_ff_chunks, x_ref, ya_ref, yb_ref, yc_ref, yd_ref, wout_ref, gpost_ref, gpre_ref,
                   wfi_ref, wfo_ref, gfpost_ref, o_ref):
    y = jnp.concatenate([ya_ref[...], yb_ref[...], yc_ref[...], yd_ref[...]], axis=-1)
    mixed = jnp.dot(y.astype(BF16), wout_ref[...], preferred_element_type=F32)
    x1 = x_ref[...] + _rms(mixed, gpost_ref[...])
    h = _rms(x1, gpre_ref[...]).astype(BF16)
    d_ff = wfo_ref.shape[0]
    fc = d_ff // n_ff_chunks
    acc = None
    for j in range(n_ff_chunks):
        gate = jnp.dot(h, wfi_ref[:, fc * j:fc * (j + 1)], preferred_element_type=F32)
        up = jnp.dot(h, wfi_ref[:, d_ff + fc * j:d_ff + fc * (j + 1)], preferred_element_type=F32)
        part = jnp.dot((_silu(gate) * up).astype(BF16), wfo_ref[fc * j:fc * (j + 1), :],
                       preferred_element_type=F32)
        acc = part if acc is None else acc + part
    o_ref[...] = x1 + _rms(acc, gfpost_ref[...])


def _outffn(x2, ys, w_out, g_post, g_pre, w_fi, w_fo, g_fpost):
    n, d = x2.shape
    d_ff = w_fo.shape[0]
    n_ff_chunks = 11
    assert (d_ff // n_ff_chunks) % LANES == 0
    tile = lambda w: pl.BlockSpec((TM_PROJ, w), lambda i: (i, 0))
    const = lambda a: pl.BlockSpec(a.shape, lambda i: (0, 0), pipeline_mode=pl.Buffered(1))
    params = [w_out, g_post, g_pre, w_fi, w_fo, g_fpost]
    return pl.pallas_call(
        functools.partial(_outffn_kernel, n_ff_chunks),
        out_shape=jax.ShapeDtypeStruct((n, d), F32),
        grid=(n // TM_PROJ,),
        in_specs=[tile(d)] + [tile(G_WIDTH)] * 4 + [const(p) for p in params],
        out_specs=tile(d),
        compiler_params=pltpu.CompilerParams(dimension_semantics=("arbitrary",),
                                             vmem_limit_bytes=VMEM_LIMIT),
        name="out_ffn",
    )(x2, *ys, *params)


def _permute_w_in(w_in):
    g = G_WIDTH
    a0, b0 = 0, 2 * g
    c0 = b0 + 4 * g + 2 * N_DIR * N_HEADS
    d0 = c0 + 4 * g + 2 * N_DIR * N_HEADS
    d_cols = w_in.shape[1] - d0
    pieces = [w_in[:, b0:b0 + 4 * g], w_in[:, c0:c0 + 4 * g], w_in[:, d0:],
              w_in[:, b0 + 4 * g:c0], w_in[:, c0 + 4 * g:d0]]
    used = 8 * g + d_cols + 4 * N_DIR * N_HEADS
    pieces.append(jnp.zeros((w_in.shape[0], OFF_LRU - used), w_in.dtype))
    pieces.append(w_in[:, a0:a0 + 2 * g])
    return jnp.concatenate(pieces, axis=1)


def _layer(x, n_mix_pre, n_mix_post, n_ffn_pre, n_ffn_post, w_in, w_out,
           lru_conv_w, lru_conv_b, lru_gate_w, lru_gate_b, lru_lambda,
           gdn_conv_w, gdn_a_log, gdn_dt_bias, gdn_norm, mlstm_gate_bias, mlstm_norm,
           rwkv_mu, rwkv_w0, rwkv_w_up, rwkv_a0, rwkv_a_up, rwkv_g_up,
           rwkv_k_k, rwkv_k_a, rwkv_r_k, rwkv_gn_w, rwkv_gn_b, ffn_w_in, ffn_w_out):
    bsz, seq, d = x.shape
    x2 = x.reshape(bsz * seq, d)
    gate_bias_row = (_gate_row(mlstm_gate_bias[:, 0], LANE_GI) + _gate_row(mlstm_gate_bias[:, 1], LANE_GF))
    mu_pad = jnp.pad(rwkv_mu, ((0, 0), (0, OFF_LRU - OFF_RWKV - rwkv_mu.shape[1])))
    u3 = _inproj(x2, seq, _row(n_mix_pre), _permute_w_in(w_in).astype(BF16),
                 _gate_row(gdn_a_log, LANE_ALPHA), _gate_row(gdn_dt_bias, LANE_ALPHA), gate_bias_row,
                 gdn_conv_w, lru_conv_w, _row(lru_conv_b), mu_pad).reshape(bsz, seq, P_PAD)
    ys = [
        _lru_mixer(u3, lru_gate_w, lru_gate_b, lru_lambda),
        _gdn_mixer(u3, gdn_norm),
        _mlstm_mixer(u3, mlstm_norm),
        _rwkv_mixer(u3, rwkv_w0, rwkv_w_up, rwkv_a0, rwkv_a_up, rwkv_g_up,
                    rwkv_k_k, rwkv_k_a, rwkv_r_k, rwkv_gn_w, rwkv_gn_b),
    ]
    ys = [y.reshape(bsz * seq, G_WIDTH) for y in ys]
    out = _outffn(x2, ys, w_out.astype(BF16), _row(n_mix_post), _row(n_ffn_pre),
                  ffn_w_in.astype(BF16), ffn_w_out.astype(BF16), _row(n_ffn_post))
    return out.reshape(bsz, seq, d)


def kernel(x, norm_mix_pre, norm_mix_post, norm_ffn_pre, norm_ffn_post, w_in, w_out, lru_conv_w, lru_conv_b, lru_gate_w, lru_gate_b, lru_lambda, gdn_conv_w, gdn_a_log, gdn_dt_bias, gdn_norm, mlstm_gate_bias, mlstm_norm, rwkv_mu, rwkv_w0, rwkv_w_up, rwkv_a0, rwkv_a_up, rwkv_g_up, rwkv_k_k, rwkv_k_a, rwkv_r_k, rwkv_gn_w, rwkv_gn_b, ffn_w_in, ffn_w_out):
    bsz, seq, d = x.shape
    assert d == N_HEADS * G_WIDTH and seq % TM_PROJ == 0
    assert all(seq % t == 0 and bsz % rows == 0 for t, rows in (BLK_GDN, BLK_MLSTM, BLK_RWKV))
    assert seq % min(T_LRU, seq) == 0
    assert w_in.shape[-1] - (10 * G_WIDTH + 4 * N_DIR * N_HEADS) == 4 * G_WIDTH - RWKV_G_RANK
    stacked = (norm_mix_pre, norm_mix_post, norm_ffn_pre, norm_ffn_post, w_in, w_out,
               lru_conv_w, lru_conv_b, lru_gate_w, lru_gate_b, lru_lambda,
               gdn_conv_w, gdn_a_log, gdn_dt_bias, gdn_norm, mlstm_gate_bias, mlstm_norm,
               rwkv_mu, rwkv_w0, rwkv_w_up, rwkv_a0, rwkv_a_up, rwkv_g_up,
               rwkv_k_k, rwkv_k_a, rwkv_r_k, rwkv_gn_w, rwkv_gn_b, ffn_w_in, ffn_w_out)
    for l in range(w_in.shape[0]):
        x = _layer(x, *(p[l] for p in stacked))
    return x
```

```python
import functools

import jax
import jax.numpy as jnp
from jax import lax
from jax.experimental import pallas as pl
from jax.experimental.pallas import tpu as pltpu

F32 = jnp.float32
BF16 = jnp.bfloat16

HEAD_DIM = 64
N_HEADS = 4
G_WIDTH = N_HEADS * HEAD_DIM
N_DIR = 2
CONV_K = 4
CHUNK = 64
CHUNK_SHIFT = 6
LRU_C = 8.0
GATE_CAP = 15.0
RWKV_RANK_COLS = 128
RWKV_G_RANK = 64
RWKV_GN_EPS = 64e-5
RMS_EPS = 1e-6
L2_EPS = 1e-6

LANES = 128
LANE_SHIFT = 7
HALO = 8
BLK_GDN = (128, 8)
BLK_MLSTM = (256, 8)
BLK_RWKV = (128, 8)
T_LRU = 1024
ROW_LAG = 1
TM_PROJ = 512
TM_FFN = 1024
VMEM_LIMIT = 56 * 1024 * 1024

OFF_GDN = 0
OFF_MLSTM = 4 * G_WIDTH
OFF_RWKV = 8 * G_WIDTH
OFF_LRU = 12 * G_WIDTH
P_PAD = 14 * G_WIDTH
GATE_TILE_BLK = (OFF_LRU - LANES) // LANES
LANE_ALPHA = RWKV_G_RANK
LANE_BETA = LANE_ALPHA + N_DIR * N_HEADS
LANE_GI = LANE_BETA + N_DIR * N_HEADS
LANE_GF = LANE_GI + N_DIR * N_HEADS


def _bdot(a, b):
    return jnp.dot(a.astype(BF16), b.astype(BF16), preferred_element_type=F32)


def _bdot_nt(a, b):
    return lax.dot_general(a.astype(BF16), b.astype(BF16), (((1,), (1,)), ((), ())),
                           preferred_element_type=F32)


def _bdot_tn(a, b):
    return lax.dot_general(a.astype(BF16), b.astype(BF16), (((0,), (0,)), ((), ())),
                           preferred_element_type=F32)


def _split_bf16(x, n):
    parts, r = [], x
    for _ in range(n):
        p = r.astype(BF16)
        parts.append(p)
        r = r - p.astype(F32)
    return parts


def _dot_sel_lhs(m_bf16, x, n=3):
    acc = None
    for p in _split_bf16(x, n):
        t = jnp.dot(m_bf16, p, preferred_element_type=F32)
        acc = t if acc is None else acc + t
    return acc


def _dot_sel_rhs(x, m_bf16, n=2):
    acc = None
    for p in _split_bf16(x, n):
        t = jnp.dot(p, m_bf16, preferred_element_type=F32)
        acc = t if acc is None else acc + t
    return acc


def _softplus(x):
    return jnp.maximum(x, 0.0) + jnp.log1p(jnp.exp(-jnp.abs(x)))


def _sigmoid(x):
    return 1.0 / (1.0 + jnp.exp(-x))


def _silu(x):
    return x * _sigmoid(x)


def _gelu_tanh(x):
    c = 0.7978845608028654
    return x * (0.5 * (1.0 + jnp.tanh(c * (x + 0.044715 * (x * x * x)))))


def _soft_cap(x):
    return GATE_CAP * jnp.tanh(x / GATE_CAP)


def _rms(x, g):
    return x * lax.rsqrt(jnp.mean(x * x, axis=-1, keepdims=True) + RMS_EPS) * g


def _head_sum_matrix(width):
    i = lax.broadcasted_iota(jnp.int32, (width, width), 0)
    j = lax.broadcasted_iota(jnp.int32, (width, width), 1)
    return jnp.where((i >> CHUNK_SHIFT) == (j >> CHUNK_SHIFT), 1.0, 0.0).astype(BF16)


def _chunk_masks(t, rev):
    i = lax.broadcasted_iota(jnp.int32, (t, t), 0)
    j = lax.broadcasted_iota(jnp.int32, (t, t), 1)
    same = (i >> CHUNK_SHIFT) == (j >> CHUNK_SHIFT)
    if rev:
        return same & (i <= j), same & (i < j)
    return same & (i >= j), same & (i > j)


def _solve_nilpotent(ms, rhss):
    xs = [r + _bdot(m, r) for m, r in zip(ms, rhss)]
    ps = list(ms)
    for _ in range(CHUNK_SHIFT - 1):
        yield
        ps = [_bdot(p, p) for p in ps]
        xs = [x + _bdot(p, x) for p, x in zip(ps, xs)]
    return xs


def _interleave(programs, lag):
    done = [False] * len(programs)
    step = 0
    while not all(done):
        for k, prog in enumerate(programs):
            if done[k] or step < lag * k:
                continue
            try:
                next(prog)
            except StopIteration:
                done[k] = True
        step += 1


def _chunk_order(t, rev):
    n = t // CHUNK
    return list(range(n - 1, -1, -1)) if rev else list(range(n))


def _inproj_kernel(tiles_per_seq, x_ref, xp_ref, xn_ref, g_ref, w_ref, alog_ref, dtb_ref, bias_ref,
                   gcw_ref, lcw_ref, lcb_ref, mu_ref, u_ref, ue_ref):
    i = pl.program_id(0)
    tm = x_ref.shape[0]
    pos = lax.rem(i, tiles_per_seq)
    xs = jnp.concatenate([jnp.where(pos == 0, 0.0, xp_ref[...]), x_ref[...],
                          jnp.where(pos == tiles_per_seq - 1, 0.0, xn_ref[...])], axis=0)
    h = _rms(xs, g_ref[...]).astype(BF16)
    body = pl.ds(HALO, tm)

    def project(c0, c1):
        if OFF_MLSTM <= c0 < OFF_RWKV:
            u_ref[:, c0:c1] = jnp.dot(h[HALO:HALO + tm], w_ref[:, c0:c1], preferred_element_type=F32)
        else:
            ue_ref[:, c0:c1] = jnp.dot(h, w_ref[:, c0:c1], preferred_element_type=F32)

    def conv(c0, c1, cw_ref, off):
        acc = None
        for j in range(CONV_K):
            term = cw_ref[j:j + 1, c0 - off:c1 - off] * ue_ref[pl.ds(HALO - CONV_K // 2 + j, tm), c0:c1]
            acc = term if acc is None else acc + term
        return acc

    def epilogue(c0, c1):
        if c1 <= OFF_GDN + 3 * G_WIDTH:
            u_ref[:, c0:c1] = _silu(conv(c0, c1, gcw_ref, OFF_GDN))
        elif OFF_RWKV <= c0 < OFF_LRU:
            cur = ue_ref[body, c0:c1]
            prv = ue_ref[pl.ds(HALO - 1, tm), c0:c1]
            nxt = ue_ref[pl.ds(HALO + 1, tm), c0:c1]
            mu0 = mu_ref[0:1, c0 - OFF_RWKV:c1 - OFF_RWKV]
            mu1 = mu_ref[1:2, c0 - OFF_RWKV:c1 - OFF_RWKV]
            us = cur + mu0 * (prv - cur) + mu1 * (nxt - cur)
            if c1 == OFF_LRU:
                raw = us[:, -LANES:]
                lane = lax.broadcasted_iota(jnp.int32, raw.shape, 1)
                capped = _soft_cap(raw + bias_ref[...])
                done = jnp.where(lane < LANE_BETA, -jnp.exp(alog_ref[...]) * _softplus(raw + dtb_ref[...]),
                                 jnp.where(lane < LANE_GI, _sigmoid(raw),
                                           jnp.where(lane < LANE_GF, capped, -_softplus(-capped))))
                us = jnp.concatenate([us[:, :-LANES], jnp.where(lane >= LANE_ALPHA, done, raw)], axis=1)
            u_ref[:, c0:c1] = us
        elif c0 == OFF_LRU:
            u_ref[:, c0:c1] = conv(c0, c1, lcw_ref, OFF_LRU) + lcb_ref[...]
        elif not OFF_MLSTM <= c0 < OFF_RWKV:
            u_ref[:, c0:c1] = ue_ref[body, c0:c1]

    piece = 2 * LANES
    order = [0, 4, 8, 1, 5, 9, 2, 6, 10, 11, 7, 12, 3, 13]
    assert sorted(order) == list(range(u_ref.shape[1] // piece)) and G_WIDTH == piece
    for n, p in enumerate(order):
        project(piece * p, piece * (p + 1))
        if n:
            epilogue(piece * order[n - 1], piece * (order[n - 1] + 1))
    epilogue(piece * order[-1], piece * (order[-1] + 1))


def _inproj(x2, seq, g, w, alog_row, dtb_row, bias_row, gdn_cw, lru_cw, lru_cb, mu_pad):
    n, d = x2.shape
    p = w.shape[1]
    per_tile = TM_PROJ // HALO
    const = lambda a: pl.BlockSpec(a.shape, lambda i: (0, 0))
    params = [g, w, alog_row, dtb_row, bias_row, gdn_cw, lru_cw, lru_cb, mu_pad]
    return pl.pallas_call(
        functools.partial(_inproj_kernel, seq // TM_PROJ),
        out_shape=jax.ShapeDtypeStruct((n, p), F32),
        grid=(n // TM_PROJ,),
        in_specs=[pl.BlockSpec((TM_PROJ, d), lambda i: (i, 0)),
                  pl.BlockSpec((HALO, d), lambda i: (jnp.maximum(i * per_tile - 1, 0), 0)),
                  pl.BlockSpec((HALO, d), lambda i: (jnp.minimum((i + 1) * per_tile, n // HALO - 1), 0))]
                 + [const(a) for a in params],
        out_specs=pl.BlockSpec((TM_PROJ, p), lambda i: (i, 0)),
        scratch_shapes=[pltpu.VMEM((TM_PROJ + 2 * HALO, p), F32)],
        compiler_params=pltpu.CompilerParams(dimension_semantics=("arbitrary",),
                                             vmem_limit_bytes=VMEM_LIMIT),
        name="in_proj",
    )(x2, x2, x2, *params)


def _lru_kernel(rev, nt, u_ref, wg_ref, gb_ref, lam_ref, *rest):
    if rev:
        hf_ref, out_ref, a_ref, b_ref, h_ref, carry_ref = rest
    else:
        out_ref, a_ref, b_ref, h_ref, carry_ref = rest
    t = u_ref.shape[1]
    i = pl.program_id(1)
    x = u_ref[0]
    xc = x[:, :G_WIDTH]
    pre = _bdot(xc, wg_ref[...]) + gb_ref[...]
    r = _sigmoid(pre[:, :G_WIDTH])
    ig = _sigmoid(pre[:, G_WIDTH:])
    log_a = -LRU_C * r * _softplus(-lam_ref[...])
    a = jnp.exp(log_a)
    th = jnp.tanh(log_a)
    b = jnp.sqrt(-2.0 * th / (1.0 - th)) * ig * xc

    pos = lax.broadcasted_iota(jnp.int32, (t, G_WIDTH), 0) & (HALO - 1)
    shift = 1
    while shift < HALO:
        if rev:
            keep = pos < HALO - shift
            a_sh = jnp.where(keep, pltpu.roll(a, t - shift, axis=0), 1.0)
            b_sh = jnp.where(keep, pltpu.roll(b, t - shift, axis=0), 0.0)
        else:
            keep = pos >= shift
            a_sh = jnp.where(keep, pltpu.roll(a, shift, axis=0), 1.0)
            b_sh = jnp.where(keep, pltpu.roll(b, shift, axis=0), 0.0)
        b = a * b_sh + b
        a = a * a_sh
        shift *= 2
    a_ref[...] = a
    b_ref[...] = b

    @pl.when(i == 0)
    def _():
        carry_ref[...] = jnp.zeros_like(carry_ref)

    n_groups = t // HALO

    def step(s, h):
        g = n_groups - 1 - s if rev else s
        rows = pl.ds(pl.multiple_of(g * HALO, HALO), HALO)
        h_grp = a_ref[rows, :] * h + b_ref[rows, :]
        h_ref[rows, :] = h_grp
        return h_grp[0:1, :] if rev else h_grp[HALO - 1:HALO, :]

    carry_ref[...] = lax.fori_loop(0, n_groups, step, carry_ref[...], unroll=4)
    if rev:
        out_ref[0] = (hf_ref[0] + h_ref[...]) * _gelu_tanh(x[:, G_WIDTH:])
    else:
        out_ref[0] = h_ref[...]


def _gdn_kernel(rev, nt, u_ref, gt_ref, ng_ref, *rest):
    if rev:
        of_ref, out_ref, s_ref, o_ref = rest
    else:
        out_ref, s_ref, o_ref = rest
    t = u_ref.shape[1]
    d = 1 if rev else 0
    i = pl.program_id(1)
    heads = range(N_HEADS)
    hsl = [slice(HEAD_DIM * h, HEAD_DIM * (h + 1)) for h in heads]
    order = _chunk_order(t, rev)

    @pl.when(i == 0)
    def _():
        s_ref[...] = jnp.zeros_like(s_ref)

    def row(r):
        x = u_ref[r]
        hsum = _head_sum_matrix(G_WIDTH)
        q = x[:, :G_WIDTH]
        k = x[:, G_WIDTH:2 * G_WIDTH]
        v = x[:, 2 * G_WIDTH:3 * G_WIDTH]
        q = q * lax.rsqrt(_dot_sel_rhs(q * q, hsum) + L2_EPS) * (HEAD_DIM ** -0.5)
        k = k * lax.rsqrt(_dot_sel_rhs(k * k, hsum) + L2_EPS)
        g_all = gt_ref[r]
        beta_all = g_all
        tri, strict = _chunk_masks(t, rev)
        tri_bf = jnp.where(tri, 1.0, 0.0).astype(BF16)
        gc_all = _dot_sel_lhs(tri_bf, g_all, n=2)
        gc_rows = gc_all.T
        yield

        gc_cs = [gc_all[:, LANE_ALPHA + N_HEADS * d + h:LANE_ALPHA + N_HEADS * d + h + 1] for h in heads]
        ms, rhss, qks, qes = [], [], [], []
        for h in heads:
            la = LANE_ALPHA + N_HEADS * d + h
            lb = LANE_BETA + N_HEADS * d + h
            beta = beta_all[:, lb:lb + 1]
            dec = jnp.exp(jnp.where(tri, gc_cs[h] - gc_rows[la:la + 1, :], -jnp.inf))
            qh, kh, vh = q[:, hsl[h]], k[:, hsl[h]], v[:, hsl[h]]
            kb = kh * beta
            e_gc = jnp.exp(gc_cs[h])
            ms.append(jnp.where(strict, -(_bdot_nt(kb, kh) * dec), 0.0))
            rhss.append(jnp.concatenate([vh * beta, kb * e_gc], axis=1))
            qks.append(_bdot_nt(qh, kh) * dec)
            qes.append(qh * e_gc)
        yield
        sols = yield from _solve_nilpotent(ms, rhss)
        yield

        terms = {}
        for c in order:
            sl = slice(CHUNK * c, CHUNK * (c + 1))
            rl = CHUNK * c if rev else CHUNK * (c + 1) - 1
            for h in heads:
                la = LANE_ALPHA + N_HEADS * d + h
                g_last = jnp.broadcast_to(gc_all[rl:rl + 1, :], (CHUNK, LANES))[:, la:la + 1]
                kd = k[sl, hsl[h]] * jnp.exp(g_last - gc_cs[h][sl])
                kuw = _bdot_tn(kd, sols[h][sl])
                quw = _bdot(qks[h][sl, sl], sols[h][sl])
                lhs = jnp.concatenate([qes[h][sl] - quw[:, HEAD_DIM:], -kuw[:, HEAD_DIM:]], axis=0)
                terms[h, c] = (lhs, quw[:, :HEAD_DIM], kuw[:, :HEAD_DIM], jnp.exp(g_last))
        yield

        states = [s_ref[N_HEADS * r + h] for h in heads]
        for c in order:
            sl = slice(CHUNK * c, CHUNK * (c + 1))
            for h in heads:
                lhs, o_const, s_const, e_last = terms[h, c]
                both = _bdot(lhs, states[h])
                o_ref[r, sl, hsl[h]] = both[:CHUNK] + o_const
                states[h] = e_last * states[h] + both[CHUNK:] + s_const
            yield
        for h in heads:
            s_ref[N_HEADS * r + h] = states[h]

        if rev:
            o = of_ref[r] + o_ref[r]
            ms_o = _dot_sel_rhs(o * o, hsum) * (1.0 / HEAD_DIM)
            out_ref[r] = o * lax.rsqrt(ms_o + RMS_EPS) * ng_ref[...] * _silu(x[:, 3 * G_WIDTH:])
        else:
            out_ref[r] = o_ref[r]

    _interleave([row(r) for r in range(u_ref.shape[0])], ROW_LAG)


def _mlstm_kernel(rev, nt, u_ref, gt_ref, ng_ref, *rest):
    if rev:
        hf_ref, out_ref, c_ref, m_ref, h_ref = rest
    else:
        out_ref, c_ref, m_ref, h_ref = rest
    t = u_ref.shape[1]
    d = 1 if rev else 0
    i = pl.program_id(1)
    heads = range(N_HEADS)
    order = _chunk_order(t, rev)
    hsl = [slice(HEAD_DIM * h, HEAD_DIM * (h + 1)) for h in heads]
    n_chunks = t // CHUNK
    assert 2 * n_chunks <= HALO and t % LANES == 0

    @pl.when(i == 0)
    def _():
        c_ref[...] = jnp.zeros_like(c_ref)
        m_ref[...] = jnp.zeros_like(m_ref)

    def row(r):
        x = u_ref[r]
        q = x[:, :G_WIDTH]
        k = x[:, G_WIDTH:2 * G_WIDTH] * (HEAD_DIM ** -0.5)
        v = x[:, 2 * G_WIDTH:3 * G_WIDTH]
        ig_all = gt_ref[r]
        lf_all = ig_all
        tri, _ = _chunk_masks(t, rev)
        tri_bf = jnp.where(tri, 1.0, 0.0).astype(BF16)
        b_all = _dot_sel_lhs(tri_bf, lf_all, n=2)
        b_rows = b_all.T
        ig_rows = ig_all.T

        ig_al = pltpu.roll(ig_all, LANE_GF - LANE_GI, axis=1)
        m_run = m_ref[r, 0:1, :]
        m_in, dec_rows, sc_tiles = {}, {}, {}
        for c in order:
            sl = slice(CHUNK * c, CHUNK * (c + 1))
            rl = CHUNK * c if rev else CHUNK * (c + 1) - 1
            b_last = b_all[rl:rl + 1, :]
            w_end = b_last - b_all[sl] + ig_al[sl]
            m_new = jnp.maximum(b_last + m_run, jnp.max(w_end, axis=0, keepdims=True))
            m_in[c] = m_run
            dec_rows[c] = jnp.exp(b_last + m_run - m_new)
            sc_tiles[c] = jnp.exp(w_end - m_new)
            m_run = m_new
        m_ref[r] = jnp.broadcast_to(m_run, m_ref.shape[1:])
        yield

        lfs = [LANE_GF + N_HEADS * d + h for h in heads]
        sel_r = lax.broadcasted_iota(jnp.int32, (LANES, N_HEADS * LANES), 0)
        sel_c = lax.broadcasted_iota(jnp.int32, (LANES, N_HEADS * LANES), 1)
        sel = jnp.where(sel_r == LANE_GF + N_HEADS * d + (sel_c >> LANE_SHIFT), 1.0, 0.0).astype(BF16)
        b_rep = _dot_sel_rhs(b_all, sel, n=2)
        small_rep = _dot_sel_rhs(
            jnp.concatenate([m_in[c] for c in range(n_chunks)] + [dec_rows[c] for c in range(n_chunks)]
                            + [jnp.zeros((HALO - 2 * n_chunks, LANES), F32)] * (2 * n_chunks < HALO),
                            axis=0), sel, n=3)
        khs = [k[:, hsl[h]] for h in heads]
        ones_blk = jnp.ones((t, HEAD_DIM), F32)
        v_exts = [jnp.concatenate([v[:, hsl[h]], ones_blk], axis=1) for h in heads]
        intras, s_inters, floors = [], [], []
        for h in heads:
            li = LANE_GI + N_HEADS * d + h
            lf = lfs[h]
            hl = slice(LANES * h, LANES * (h + 1))
            b_h = b_rep[:, hl]
            inter = jnp.concatenate([b_h[CHUNK * c:CHUNK * (c + 1)] + small_rep[c:c + 1, hl]
                                     for c in range(n_chunks)], axis=0)
            d_log = jnp.where(tri, jnp.tile(b_h, (1, t // LANES)) - b_rows[lf:lf + 1, :]
                              + ig_rows[li:li + 1, :], -jnp.inf)
            m_t = jnp.maximum(inter, jnp.max(d_log, axis=1, keepdims=True))
            p = _bdot_nt(q[:, hsl[h]], khs[h]) * jnp.exp(d_log - jnp.tile(m_t, (1, t // LANES)))
            intras.append(_bdot(p, v_exts[h]))
            s_inters.append(jnp.exp(inter - m_t))
            floors.append(jnp.exp(-m_t))
            if h % 2 == 1:
                yield
        incs, decs = {}, {}
        for c in order:
            sl = slice(CHUNK * c, CHUNK * (c + 1))
            for h in heads:
                lf = lfs[h]
                decs[h, c] = small_rep[n_chunks + c:n_chunks + c + 1, LANES * h:LANES * (h + 1)]
                incs[h, c] = _bdot_tn(sc_tiles[c][:, lf:lf + 1] * khs[h][sl], v_exts[h][sl])
        yield

        cns = [c_ref[N_HEADS * r + h] for h in heads]
        for c in order:
            sl = slice(CHUNK * c, CHUNK * (c + 1))
            for h in heads:
                nd = intras[h][sl] + s_inters[h][sl] * _bdot(q[sl, hsl[h]], cns[h])
                den = jnp.maximum(jnp.abs(pltpu.roll(nd, HEAD_DIM, axis=1)), floors[h][sl])
                h_ref[r, sl, hsl[h]] = (nd / den)[:, :HEAD_DIM]
                cns[h] = decs[h, c] * cns[h] + incs[h, c]
            yield
        for h in heads:
            c_ref[N_HEADS * r + h] = cns[h]

        if rev:
            hh = hf_ref[r] + h_ref[r]
            ms = _dot_sel_rhs(hh * hh, _head_sum_matrix(G_WIDTH)) * (1.0 / HEAD_DIM)
            out_ref[r] = hh * lax.rsqrt(ms + RMS_EPS) * ng_ref[...] * _sigmoid(x[:, 3 * G_WIDTH:])
        else:
            out_ref[r] = h_ref[r]

    _interleave([row(r) for r in range(u_ref.shape[0])], ROW_LAG)


def _rwkv_kernel(rev, nt, u_ref, wup_ref, w0_ref, a0_ref, gup_ref,
                 kk_ref, ka_ref, rk_ref, gnw_ref, gnb_ref, *rest):
    if rev:
        yf_ref, out_ref, s_ref, y_ref = rest
    else:
        out_ref, s_ref, y_ref = rest
    t = u_ref.shape[1]
    i = pl.program_id(1)
    heads = range(N_HEADS)
    hsl = [slice(HEAD_DIM * h, HEAD_DIM * (h + 1)) for h in heads]
    order = _chunk_order(t, rev)

    @pl.when(i == 0)
    def _():
        s_ref[...] = jnp.zeros_like(s_ref)

    def row(r):
        us = u_ref[r]
        rr = us[:, :G_WIDTH]
        k = us[:, G_WIDTH:2 * G_WIDTH]
        v = us[:, 2 * G_WIDTH:3 * G_WIDTH]
        low = us[:, 3 * G_WIDTH:3 * G_WIDTH + RWKV_RANK_COLS]
        lane = lax.broadcasted_iota(jnp.int32, (t, RWKV_RANK_COLS), 1)
        low = jnp.where(lane < RWKV_RANK_COLS // 2, jnp.tanh(low), low)
        za = _bdot(low, wup_ref[...])
        z_w = w0_ref[...] + za[:, :G_WIDTH]
        lw = -jnp.exp(-_softplus(-z_w) - 0.5)
        a = _sigmoid(a0_ref[...] + za[:, G_WIDTH:])
        gate = _bdot(_sigmoid(us[:, 3 * G_WIDTH + RWKV_RANK_COLS:]), gup_ref[...])
        hsum = _head_sum_matrix(G_WIDTH)
        kk = k * kk_ref[...]
        kk = kk * lax.rsqrt(_dot_sel_rhs(kk * kk, hsum) + L2_EPS)
        k_dir = k * (1.0 + (a - 1.0) * ka_ref[...])
        akk = a * kk
        yield

        tri, strict = _chunk_masks(t, rev)
        tri_bf = jnp.where(tri, 1.0, 0.0).astype(BF16)
        gc = _dot_sel_lhs(tri_bf, lw, n=2)
        e_inc = jnp.exp(gc)
        e_exc = jnp.exp(gc - lw)
        e_neg = jnp.exp(-gc)
        left_a = -kk * e_exc
        left_r = rr * e_inc
        right_b = akk * e_neg
        right_k = k_dir * e_neg
        yield

        a_abs, a_rbs, rhss, yvs = [], [], [], []
        for h in heads:
            hs = hsl[h]
            gram = _bdot_nt(jnp.concatenate([left_a[:, hs], left_r[:, hs]], axis=0),
                            jnp.concatenate([right_b[:, hs], right_k[:, hs]], axis=0))
            a_abs.append(jnp.where(strict, gram[:t, :t], 0.0))
            a_rbs.append(jnp.where(tri, gram[t:, :t], 0.0))
            a_k = jnp.concatenate([jnp.where(strict, gram[:t, t:], 0.0),
                                   jnp.where(tri, gram[t:, t:], 0.0)], axis=0)
            akv = _bdot(a_k, v[:, hs])
            rhss.append(jnp.concatenate([left_a[:, hs], akv[:t]], axis=1))
            yvs.append(akv[t:])
            if h % 2 == 1:
                yield
        sols = yield from _solve_nilpotent(a_abs, rhss)
        yield

        terms = {}
        for c in order:
            sl = slice(CHUNK * c, CHUNK * (c + 1))
            rl = CHUNK * c if rev else CHUNK * (c + 1) - 1
            for h in heads:
                hs = hsl[h]
                sol_c = sols[h][sl]
                g_last = gc[rl:rl + 1, hs]
                tail = jnp.exp(g_last - gc[sl, hs])
                bd = akk[sl, hs] * tail
                kd = k_dir[sl, hs] * tail
                ar = _bdot(a_rbs[h][sl, sl], sol_c)
                y_lhs = left_r[sl, hs] + ar[:, :HEAD_DIM]
                y_const = ar[:, HEAD_DIM:] + yvs[h][sl]
                phi = _bdot_tn(sol_c[:, :HEAD_DIM], bd)
                s_const = _bdot_tn(jnp.concatenate([sol_c[:, HEAD_DIM:], v[sl, hs]], axis=0),
                                   jnp.concatenate([bd, kd], axis=0))
                terms[h, c] = (y_lhs, y_const, phi, s_const, jnp.exp(g_last))
        yield

        states = [s_ref[N_HEADS * r + h] for h in heads]
        for c in order:
            sl = slice(CHUNK * c, CHUNK * (c + 1))
            for h in heads:
                y_lhs, y_const, phi, s_const, e_last = terms[h, c]
                y_ref[r, sl, hsl[h]] = _bdot_nt(y_lhs, states[h]) + y_const
                states[h] = states[h] * e_last + _bdot(states[h], phi) + s_const
            yield
        for h in heads:
            s_ref[N_HEADS * r + h] = states[h]

        if rev:
            y = yf_ref[r] + y_ref[r]
            mean = _dot_sel_rhs(y, hsum) * (1.0 / HEAD_DIM)
            dlt = y - mean
            var = _dot_sel_rhs(dlt * dlt, hsum) * (1.0 / HEAD_DIM)
            yn = dlt * lax.rsqrt(var + RWKV_GN_EPS) * gnw_ref[...] + gnb_ref[...]
            bonus = _dot_sel_rhs(rr * k * rk_ref[...], hsum) * v
            out_ref[r] = (yn + bonus) * gate
        else:
            out_ref[r] = y_ref[r]

    _interleave([row(r) for r in range(u_ref.shape[0])], ROW_LAG)


def _mixer_call(body, rev, rows, t, u3, col_blk, col_w, use_gates, params, prev_out, scratch, name):
    bsz, seq, _ = u3.shape
    nt = seq // t

    def blk_of(i):
        return nt - 1 - i if rev else i

    in_specs = [pl.BlockSpec((rows, t, col_w), lambda b, i: (b, blk_of(i), col_blk))]
    args = [u3]
    if use_gates:
        in_specs.append(pl.BlockSpec((rows, t, LANES), lambda b, i: (b, blk_of(i), GATE_TILE_BLK)))
        args.append(u3)
    for p in params:
        in_specs.append(pl.BlockSpec(p.shape, lambda b, i, nd=p.ndim: (0,) * nd))
        args.append(p)
    if rev:
        in_specs.append(pl.BlockSpec((rows, t, G_WIDTH), lambda b, i: (b, blk_of(i), 0)))
        args.append(prev_out)
    return pl.pallas_call(
        functools.partial(body, rev, nt),
        out_shape=jax.ShapeDtypeStruct((bsz, seq, G_WIDTH), F32),
        grid=(bsz // rows, nt),
        in_specs=in_specs,
        out_specs=pl.BlockSpec((rows, t, G_WIDTH), lambda b, i: (b, blk_of(i), 0)),
        scratch_shapes=scratch,
        compiler_params=pltpu.CompilerParams(dimension_semantics=("arbitrary", "arbitrary"),
                                             vmem_limit_bytes=VMEM_LIMIT),
        name=name + ("_bwd" if rev else "_fwd"),
    )(*args)


def _both_dirs(body, rows, t, u3, col_blk, col_w, use_gates, params_of_dir, scratch, name):
    fwd = _mixer_call(body, False, rows, t, u3, col_blk, col_w, use_gates, params_of_dir(0), None,
                      scratch, name)
    return _mixer_call(body, True, rows, t, u3, col_blk, col_w, use_gates, params_of_dir(1), fwd,
                       scratch, name)


def _row(v):
    return v.reshape(1, -1).astype(F32)


def _gate_row(values, lane0):
    flat = values.reshape(-1).astype(F32)
    return jnp.zeros((1, LANES), F32).at[0, lane0:lane0 + flat.shape[0]].set(flat)


def _lru_mixer(u3, gate_w, gate_b, lam):
    t = min(T_LRU, u3.shape[1])
    eye = jnp.eye(N_HEADS, dtype=F32)

    def params(d):
        dense = jnp.einsum('gnio,nm->nigmo', gate_w[d], eye).reshape(G_WIDTH, 2 * G_WIDTH)
        return [dense.astype(BF16), _row(gate_b[d]), _row(lam[d])]

    scratch = [pltpu.VMEM((t, G_WIDTH), F32), pltpu.VMEM((t, G_WIDTH), F32), pltpu.VMEM((t, G_WIDTH), F32),
               pltpu.VMEM((1, G_WIDTH), F32)]
    return _both_dirs(_lru_kernel, 1, t, u3, OFF_LRU // (2 * G_WIDTH), 2 * G_WIDTH, False,
                      params, scratch, "rglru")


def _gdn_mixer(u3, norm_g):
    t, rows = BLK_GDN
    shared = [_row(jnp.tile(norm_g, N_HEADS))]
    scratch = [pltpu.VMEM((rows * N_HEADS, HEAD_DIM, HEAD_DIM), F32), pltpu.VMEM((rows, t, G_WIDTH), F32)]
    return _both_dirs(_gdn_kernel, rows, t, u3, OFF_GDN // (4 * G_WIDTH), 4 * G_WIDTH, True,
                      lambda d: shared, scratch, "gdn")


def _mlstm_mixer(u3, norm_g):
    t, rows = BLK_MLSTM
    shared = [_row(norm_g)]
    scratch = [pltpu.VMEM((rows * N_HEADS, HEAD_DIM, 2 * HEAD_DIM), F32),
               pltpu.VMEM((rows, HALO, LANES), F32), pltpu.VMEM((rows, t, G_WIDTH), F32)]
    return _both_dirs(_mlstm_kernel, rows, t, u3, OFF_MLSTM // (4 * G_WIDTH), 4 * G_WIDTH, True,
                      lambda d: shared, scratch, "mlstm")


def _rwkv_mixer(u3, w0, w_up, a0, a_up, g_up, k_k, k_a, r_k, gn_w, gn_b):
    t, rows = BLK_RWKV
    rank = w_up.shape[1]
    gup_pad = jnp.pad(g_up, ((0, LANES - g_up.shape[0]), (0, 0))).astype(BF16)

    def params(d):
        wup = jnp.zeros((RWKV_RANK_COLS, 2 * G_WIDTH), F32)
        wup = wup.at[rank * d:rank * (d + 1), :G_WIDTH].set(w_up[d])
        half = RWKV_RANK_COLS // 2
        wup = wup.at[half + rank * d:half + rank * (d + 1), G_WIDTH:].set(a_up[d])
        return [wup.astype(BF16), _row(w0[d]), _row(a0[d]), gup_pad, _row(k_k), _row(k_a),
                _row(r_k), _row(gn_w), _row(gn_b)]

    scratch = [pltpu.VMEM((rows * N_HEADS, HEAD_DIM, HEAD_DIM), F32), pltpu.VMEM((rows, t, G_WIDTH), F32)]
    return _both_dirs(_rwkv_kernel, rows, t, u3, OFF_RWKV // (4 * G_WIDTH), 4 * G_WIDTH, False,
                      params, scratch, "rwkv")


def _outffn_kernel(n_ff_chunks, x_ref, ya_ref, yb_ref, yc_ref, yd_ref, wout_ref, gpost_ref, gpre_ref,
                   wfi_ref, wfo_ref, gfpost_ref, o_ref):
    y = jnp.concatenate([ya_ref[...], yb_ref[...], yc_ref[...], yd_ref[...]], axis=-1)
    mixed = jnp.dot(y.astype(BF16), wout_ref[...], preferred_element_type=F32)
    x1 = x_ref[...] + _rms(mixed, gpost_ref[...])
    h = _rms(x1, gpre_ref[...]).astype(BF16)
    d_ff = wfo_ref.shape[0]
    fc = d_ff // n_ff_chunks
    acc = None
    for j in range(n_ff_chunks):
        gate = jnp.dot(h, wfi_ref[:, fc * j:fc * (j + 1)], preferred_element_type=F32)
        up = jnp.dot(h, wfi_ref[:, d_ff + fc * j:d_ff + fc * (j + 1)], preferred_element_type=F32)
        part = jnp.dot((_silu(gate) * up).astype(BF16), wfo_ref[fc * j:fc * (j + 1), :],
                       preferred_element_type=F32)
        acc = part if acc is None else acc + part
    o_ref[...] = x1 + _rms(acc, gfpost_ref[...])


def _outffn(x2, ys, w_out, g_post, g_pre, w_fi, w_fo, g_fpost):
    n, d = x2.shape
    d_ff = w_fo.shape[0]
    n_ff_chunks = 11
    assert (d_ff // n_ff_chunks) % LANES == 0
    tm = min(TM_FFN, n)
    tile = lambda w: pl.BlockSpec((tm, w), lambda i: (i, 0))
    const = lambda a: pl.BlockSpec(a.shape, lambda i: (0, 0), pipeline_mode=pl.Buffered(1))
    params = [w_out, g_post, g_pre, w_fi, w_fo, g_fpost]
    return pl.pallas_call(
        functools.partial(_outffn_kernel, n_ff_chunks),
        out_shape=jax.ShapeDtypeStruct((n, d), F32),
        grid=(n // tm,),
        in_specs=[tile(d)] + [tile(G_WIDTH)] * 4 + [const(p) for p in params],
        out_specs=tile(d),
        compiler_params=pltpu.CompilerParams(dimension_semantics=("arbitrary",),
                                             vmem_limit_bytes=VMEM_LIMIT),
        name="out_ffn",
    )(x2, *ys, *params)


def _permute_w_in(w_in):
    g = G_WIDTH
    a0, b0 = 0, 2 * g
    c0 = b0 + 4 * g + 2 * N_DIR * N_HEADS
    d0 = c0 + 4 * g + 2 * N_DIR * N_HEADS
    d_cols = w_in.shape[1] - d0
    pieces = [w_in[:, b0:b0 + 4 * g], w_in[:, c0:c0 + 4 * g], w_in[:, d0:],
              w_in[:, b0 + 4 * g:c0], w_in[:, c0 + 4 * g:d0]]
    used = 8 * g + d_cols + 4 * N_DIR * N_HEADS
    pieces.append(jnp.zeros((w_in.shape[0], OFF_LRU - used), w_in.dtype))
    pieces.append(w_in[:, a0:a0 + 2 * g])
    return jnp.concatenate(pieces, axis=1)


def _layer(x, n_mix_pre, n_mix_post, n_ffn_pre, n_ffn_post, w_in, w_out,
           lru_conv_w, lru_conv_b, lru_gate_w, lru_gate_b, lru_lambda,
           gdn_conv_w, gdn_a_log, gdn_dt_bias, gdn_norm, mlstm_gate_bias, mlstm_norm,
           rwkv_mu, rwkv_w0, rwkv_w_up, rwkv_a0, rwkv_a_up, rwkv_g_up,
           rwkv_k_k, rwkv_k_a, rwkv_r_k, rwkv_gn_w, rwkv_gn_b, ffn_w_in, ffn_w_out):
    bsz, seq, d = x.shape
    x2 = x.reshape(bsz * seq, d)
    gate_bias_row = (_gate_row(mlstm_gate_bias[:, 0], LANE_GI) + _gate_row(mlstm_gate_bias[:, 1], LANE_GF))
    mu_pad = jnp.pad(rwkv_mu, ((0, 0), (0, OFF_LRU - OFF_RWKV - rwkv_mu.shape[1])))
    u3 = _inproj(x2, seq, _row(n_mix_pre), _permute_w_in(w_in).astype(BF16),
                 _gate_row(gdn_a_log, LANE_ALPHA), _gate_row(gdn_dt_bias, LANE_ALPHA), gate_bias_row,
                 gdn_conv_w, lru_conv_w, _row(lru_conv_b), mu_pad).reshape(bsz, seq, P_PAD)
    ys = [
        _lru_mixer(u3, lru_gate_w, lru_gate_b, lru_lambda),
        _gdn_mixer(u3, gdn_norm),
        _mlstm_mixer(u3, mlstm_norm),
        _rwkv_mixer(u3, rwkv_w0, rwkv_w_up, rwkv_a0, rwkv_a_up, rwkv_g_up,
                    rwkv_k_k, rwkv_k_a, rwkv_r_k, rwkv_gn_w, rwkv_gn_b),
    ]
    ys = [y.reshape(bsz * seq, G_WIDTH) for y in ys]
    out = _outffn(x2, ys, w_out.astype(BF16), _row(n_mix_post), _row(n_ffn_pre),
                  ffn_w_in.astype(BF16), ffn_w_out.astype(BF16), _row(n_ffn_post))
    return out.reshape(bsz, seq, d)


def kernel(x, norm_mix_pre, norm_mix_post, norm_ffn_pre, norm_ffn_post, w_in, w_out, lru_conv_w, lru_conv_b, lru_gate_w, lru_gate_b, lru_lambda, gdn_conv_w, gdn_a_log, gdn_dt_bias, gdn_norm, mlstm_gate_bias, mlstm_norm, rwkv_mu, rwkv_w0, rwkv_w_up, rwkv_a0, rwkv_a_up, rwkv_g_up, rwkv_k_k, rwkv_k_a, rwkv_r_k, rwkv_gn_w, rwkv_gn_b, ffn_w_in, ffn_w_out):
    bsz, seq, d = x.shape
    assert d == N_HEADS * G_WIDTH and seq % TM_PROJ == 0
    assert all(seq % t == 0 and bsz % rows == 0 for t, rows in (BLK_GDN, BLK_MLSTM, BLK_RWKV))
    assert seq % min(T_LRU, seq) == 0
    assert w_in.shape[-1] - (10 * G_WIDTH + 4 * N_DIR * N_HEADS) == 4 * G_WIDTH - RWKV_G_RANK
    stacked = (norm_mix_pre, norm_mix_post, norm_ffn_pre, norm_ffn_post, w_in, w_out,
               lru_conv_w, lru_conv_b, lru_gate_w, lru_gate_b, lru_lambda,
               gdn_conv_w, gdn_a_log, gdn_dt_bias, gdn_norm, mlstm_gate_bias, mlstm_norm,
               rwkv_mu, rwkv_w0, rwkv_w_up, rwkv_a0, rwkv_a_up, rwkv_g_up,
               rwkv_k_k, rwkv_k_a, rwkv_r_k, rwkv_gn_w, rwkv_gn_b, ffn_w_in, ffn_w_out)
    for l in range(w_in.shape[0]):
        x = _layer(x, *(p[l] for p in stacked))
    return x
```
